```python
import math, functools
import jax, jax.numpy as jnp
from jax import lax
import numpy as np

D_MODEL = 1024
BATCH = 8
SEQ = 8192
DEPTH = 4
DEC_BATCH = 8
DEC_SEQ = 16
PAST_LEN = 2048

CHUNK = 64
HEAD_DIM = 64
N_A_LAYERS = DEPTH // 2
N_B_LAYERS = DEPTH - N_A_LAYERS
N_HEADS_A = 16
N_KV_A = 4
WINDOW = 128
A_BACK = WINDOW // CHUNK
N_HEADS_B = 16
B_BACK = 8
B_REACH = B_BACK * CHUNK
REL_CLIP = 128
N_BUCKETS = 32
T5_MAX_DIST = 128
D_FF = 2816
CONV_W = 3
LN_EPS = 1e-5
DEEPNORM_ALPHA = (2.0 * DEPTH) ** 0.25
DEEPNORM_BETA = (8.0 * DEPTH) ** -0.25
ATTN_SCALE = HEAD_DIM ** -0.5
NEG_INF = -1e30

kernel_name = 'yoco_streaming_swa_chunkband_encoder'


def layer_norm(x, g, b):
    xf = x.astype(jnp.float32)
    mu = xf.mean(-1, keepdims=True)
    var = jnp.square(xf - mu).mean(-1, keepdims=True)
    y = (xf - mu) * lax.rsqrt(var + LN_EPS)
    return (y * g.astype(jnp.float32) + b.astype(jnp.float32)).astype(x.dtype)


def t5_bias(table, qpos, kpos):
    rel = kpos[None, :] - qpos[:, None]
    nb = N_BUCKETS // 2
    max_exact = nb // 2
    n = jnp.abs(rel)
    large = max_exact + (jnp.log(jnp.maximum(n, 1).astype(jnp.float32) / max_exact)
                         / math.log(T5_MAX_DIST / max_exact) * (nb - max_exact)).astype(jnp.int32)
    large = jnp.minimum(large, nb - 1)
    bucket = jnp.where(rel > 0, nb, 0) + jnp.where(n < max_exact, n, large)
    return jnp.transpose(table[bucket], (2, 0, 1))


def relclip_bias(table, qpos, kpos):
    d = jnp.clip(qpos[:, None] - kpos[None, :], -REL_CLIP, REL_CLIP) + REL_CLIP
    return table[:, d]


def band_attend(q, k, v, qpos, kpos, bias, sinks, n_back):
    B, Lq, H, hd = q.shape
    Lk, KV = k.shape[1], k.shape[2]
    G = H // KV
    qg = q.reshape(B, Lq, KV, G, hd)
    s = jnp.einsum('bqkgd,bskd->bkgqs', qg, k).astype(jnp.float32) * ATTN_SCALE
    s = s + bias.reshape(KV, G, Lq, Lk).astype(jnp.float32)
    qc = qpos // CHUNK
    kc = kpos // CHUNK
    valid = (kpos[None, :] >= 0) & (kc[None, :] <= qc[:, None]) & (kc[None, :] >= qc[:, None] - n_back)
    s = jnp.where(valid, s, NEG_INF)
    if sinks is None:
        p = jax.nn.softmax(s, axis=-1)
    else:
        sink = sinks.astype(jnp.float32).reshape(KV, G, 1, 1)
        m = jnp.maximum(s.max(-1, keepdims=True), sink)
        e = jnp.exp(s - m)
        p = e / (e.sum(-1, keepdims=True) + jnp.exp(sink - m))
    o = jnp.einsum('bkgqs,bskd->bqkgd', p.astype(v.dtype), v)
    return o.reshape(B, Lq, H * hd)


def attn_prompt(q, k, v, n_back, bias_fn, sinks):
    B, S, H, hd = q.shape
    n_chunks = S // CHUNK
    band = (n_back + 1) * CHUNK
    pad = n_back * CHUNK
    kp = jnp.pad(k, ((0, 0), (pad, 0), (0, 0), (0, 0)))
    vp = jnp.pad(v, ((0, 0), (pad, 0), (0, 0), (0, 0)))

    def one_chunk(ci):
        start = ci * CHUNK
        qc = lax.dynamic_slice_in_dim(q, start, CHUNK, axis=1)
        kc = lax.dynamic_slice_in_dim(kp, start, band, axis=1)
        vc = lax.dynamic_slice_in_dim(vp, start, band, axis=1)
        qpos = start + jnp.arange(CHUNK, dtype=jnp.int32)
        kpos = start - pad + jnp.arange(band, dtype=jnp.int32)
        return band_attend(qc, kc, vc, qpos, kpos, bias_fn(qpos, kpos), sinks, n_back)

    o = lax.map(one_chunk, jnp.arange(n_chunks, dtype=jnp.int32))
    return jnp.swapaxes(o, 0, 1).reshape(B, S, H * hd)


def attn_sample(q, k_new, v_new, k_cache, v_cache, n_back, bias_fn, sinks):
    B, T, H, hd = q.shape
    P = k_cache.shape[1]
    k = jnp.concatenate([k_cache.astype(k_new.dtype), k_new], axis=1)
    v = jnp.concatenate([v_cache.astype(v_new.dtype), v_new], axis=1)
    qpos = PAST_LEN + jnp.arange(T, dtype=jnp.int32)
    kpos = PAST_LEN - P + jnp.arange(P + T, dtype=jnp.int32)
    return band_attend(q, k, v, qpos, kpos, bias_fn(qpos, kpos), sinks, n_back)


def conv_ffn(h, prev, w_up, conv_w, conv_b, w_down):
    u = h @ w_up
    B, T, C = u.shape
    if prev is None:
        prev = jnp.zeros((B, CONV_W - 1, C), u.dtype)
    ue = jnp.concatenate([prev.astype(u.dtype), u], axis=1)
    y = conv_b
    for tap in range(CONV_W):
        y = y + conv_w[tap] * ue[:, tap:tap + T]
    a, g = jnp.split(y, 2, axis=-1)
    out = (jax.nn.gelu(g) * a) @ w_down
    return out, ue[:, T:]


def run_group(x, c, cache_a_k, cache_a_v, cache_b_k, cache_b_v, state_conv,
              w_ada, b_ada, ln_g, ln_b, w_qkv_a, w_o_a, sinks_a, t5_table,
              w_ada_kv, b_ada_kv, w_kv_b, w_q_b, w_o_b, relpos_b,
              w_up, conv_w, conv_b, w_down):
    prompt = cache_a_k is None
    B, T, _ = x.shape
    sc = jax.nn.silu(c)
    new_ak, new_av, new_conv = [], [], []
    k_b = v_b = None
    qa = N_HEADS_A * HEAD_DIM
    ka = N_KV_A * HEAD_DIM
    for l in range(DEPTH):
        ada = (sc @ w_ada[l] + b_ada[l])[:, None, :]
        sh_m, sc_m, g_m, sh_f, sc_f, g_f = jnp.split(ada, 6, axis=-1)
        h = x * (1 + sc_m) + sh_m
        if l < N_A_LAYERS:
            qkv = h @ w_qkv_a[l]
            q = qkv[..., :qa].reshape(B, T, N_HEADS_A, HEAD_DIM)
            k = qkv[..., qa:qa + ka].reshape(B, T, N_KV_A, HEAD_DIM)
            v = qkv[..., qa + ka:].reshape(B, T, N_KV_A, HEAD_DIM)
            bias_fn = functools.partial(t5_bias, t5_table)
            if prompt:
                o = attn_prompt(q, k, v, A_BACK, bias_fn, sinks_a[l])
                rows = min(WINDOW, T)
                new_ak.append(k[:, T - rows:])
                new_av.append(v[:, T - rows:])
            else:
                o = attn_sample(q, k, v, cache_a_k[l], cache_a_v[l], A_BACK, bias_fn, sinks_a[l])
                new_ak.append(k)
                new_av.append(v)
            mix = o @ w_o_a[l]
        else:
            j = l - N_A_LAYERS
            q = (h @ w_q_b[j]).reshape(B, T, N_HEADS_B, HEAD_DIM)
            bias_fn = functools.partial(relclip_bias, relpos_b[j])
            if prompt:
                o = attn_prompt(q, k_b, v_b, B_BACK, bias_fn, None)
            else:
                o = attn_sample(q, k_b, v_b, cache_b_k, cache_b_v, B_BACK, bias_fn, None)
            mix = o @ w_o_b[j]
        x = layer_norm(DEEPNORM_ALPHA * x + (1 + g_m) * mix, ln_g[l, 0], ln_b[l, 0])
        h = x * (1 + sc_f) + sh_f
        f, conv_rows = conv_ffn(h, None if prompt else state_conv[l], w_up[l], conv_w[l], conv_b[l], w_down[l])
        new_conv.append(conv_rows)
        x = layer_norm(DEEPNORM_ALPHA * x + (1 + g_f) * f, ln_g[l, 1], ln_b[l, 1])
        if l == N_A_LAYERS - 1:
            ada_kv = (sc @ w_ada_kv + b_ada_kv)[:, None, :]
            sh_kv, sc_kv = jnp.split(ada_kv, 2, axis=-1)
            kv = ((x * (1 + sc_kv) + sh_kv) @ w_kv_b).reshape(B, T, 2, N_HEADS_B, HEAD_DIM)
            k_b = kv[:, :, 0]
            v_b = kv[:, :, 1]
    if prompt:
        rows = min(B_REACH, T)
        b_k_rows = k_b[:, T - rows:]
        b_v_rows = v_b[:, T - rows:]
    else:
        b_k_rows = k_b
        b_v_rows = v_b
    return (x, jnp.stack(new_ak), jnp.stack(new_av), b_k_rows, b_v_rows, jnp.stack(new_conv))


def setup_inputs(seed: int = 0) -> dict:
    key = jax.random.key(seed)
    ks = jax.random.split(key, 32)
    f32 = jnp.float32

    def nrm(k, shape, s):
        return jax.random.normal(k, shape, f32) * s

    a_rows = min(WINDOW, PAST_LEN)
    b_rows = min(B_REACH, PAST_LEN)
    beta = DEEPNORM_BETA
    return {
        'x_prompt': nrm(ks[0], (BATCH, SEQ, D_MODEL), 1.0),
        'x_sample': nrm(ks[1], (DEC_BATCH, DEC_SEQ, D_MODEL), 1.0),
        'c_prompt': nrm(ks[2], (BATCH, D_MODEL), 1.0),
        'c_sample': nrm(ks[3], (DEC_BATCH, D_MODEL), 1.0),
        'cache_a_k': nrm(ks[4], (N_A_LAYERS, DEC_BATCH, a_rows, N_KV_A, HEAD_DIM), 1.0),
        'cache_a_v': nrm(ks[5], (N_A_LAYERS, DEC_BATCH, a_rows, N_KV_A, HEAD_DIM), 1.0),
        'cache_b_k': nrm(ks[6], (DEC_BATCH, b_rows, N_HEADS_B, HEAD_DIM), 1.0),
        'cache_b_v': nrm(ks[7], (DEC_BATCH, b_rows, N_HEADS_B, HEAD_DIM), 1.0),
        'state_conv': nrm(ks[8], (DEPTH, DEC_BATCH, CONV_W - 1, 2 * D_FF), 1.0),
        'w_ada': nrm(ks[9], (DEPTH, D_MODEL, 6 * D_MODEL), 0.1 * D_MODEL ** -0.5),
        'b_ada': nrm(ks[10], (DEPTH, 6 * D_MODEL), 0.02),
        'ln_g': 1.0 + nrm(ks[11], (DEPTH, 2, D_MODEL), 0.02),
        'ln_b': nrm(ks[12], (DEPTH, 2, D_MODEL), 0.02),
        'w_qkv_a': nrm(ks[13], (N_A_LAYERS, D_MODEL, (N_HEADS_A + 2 * N_KV_A) * HEAD_DIM), D_MODEL ** -0.5),
        'w_o_a': nrm(ks[14], (N_A_LAYERS, N_HEADS_A * HEAD_DIM, D_MODEL), beta * (N_HEADS_A * HEAD_DIM) ** -0.5),
        'sinks_a': nrm(ks[15], (N_A_LAYERS, N_HEADS_A), 0.5),
        't5_table': nrm(ks[16], (N_BUCKETS, N_HEADS_A), 0.5),
        'w_ada_kv': nrm(ks[17], (D_MODEL, 2 * D_MODEL), 0.1 * D_MODEL ** -0.5),
        'b_ada_kv': nrm(ks[18], (2 * D_MODEL,), 0.02),
        'w_kv_b': nrm(ks[19], (D_MODEL, 2 * N_HEADS_B * HEAD_DIM), D_MODEL ** -0.5),
        'w_q_b': nrm(ks[20], (N_B_LAYERS, D_MODEL, N_HEADS_B * HEAD_DIM), D_MODEL ** -0.5),
        'w_o_b': nrm(ks[21], (N_B_LAYERS, N_HEADS_B * HEAD_DIM, D_MODEL), beta * (N_HEADS_B * HEAD_DIM) ** -0.5),
        'relpos_b': nrm(ks[22], (N_B_LAYERS, N_HEADS_B, 2 * REL_CLIP + 1), 0.5),
        'w_up': nrm(ks[23], (DEPTH, D_MODEL, 2 * D_FF), D_MODEL ** -0.5),
        'conv_w': nrm(ks[24], (DEPTH, CONV_W, 2 * D_FF), CONV_W ** -0.5),
        'conv_b': nrm(ks[25], (DEPTH, 2 * D_FF), 0.02),
        'w_down': nrm(ks[26], (DEPTH, D_FF, D_MODEL), beta * D_FF ** -0.5),
    }


def reference(x_prompt, x_sample, c_prompt, c_sample, cache_a_k, cache_a_v, cache_b_k, cache_b_v,
              state_conv, w_ada, b_ada, ln_g, ln_b, w_qkv_a, w_o_a, sinks_a, t5_table,
              w_ada_kv, b_ada_kv, w_kv_b, w_q_b, w_o_b, relpos_b, w_up, conv_w, conv_b, w_down):
    weights = (w_ada, b_ada, ln_g, ln_b, w_qkv_a, w_o_a, sinks_a, t5_table,
               w_ada_kv, b_ada_kv, w_kv_b, w_q_b, w_o_b, relpos_b, w_up, conv_w, conv_b, w_down)
    y_prompt, p_ak, p_av, p_bk, p_bv, p_conv = run_group(
        x_prompt, c_prompt, None, None, None, None, None, *weights)
    y_sample, s_ak, s_av, s_bk, s_bv, s_conv = run_group(
        x_sample, c_sample, cache_a_k, cache_a_v, cache_b_k, cache_b_v, state_conv, *weights)
    return (y_prompt, y_sample, p_ak, p_av, p_bk, p_bv, p_conv, s_ak, s_av, s_bk, s_bv, s_conv)
```

```python
import functools
import math

import numpy as np
import jax
import jax.numpy as jnp
from jax import lax
from jax.experimental import pallas as pl
from jax.experimental.pallas import tpu as pltpu

F32 = jnp.float32
BF16 = jnp.bfloat16

D_MODEL = 1024
DEPTH = 4
PAST_LEN = 2048
CHUNK = 64
HEAD_DIM = 64
N_A_LAYERS = DEPTH // 2
N_HEADS = 16
N_KV_A = 4
A_BACK = 2
B_BACK = 8
REL_CLIP = 128
N_BUCKETS = 32
T5_MAX_DIST = 128
D_FF = 2816
CONV_W = 3
LN_EPS = 1e-5
DEEPNORM_ALPHA = (2.0 * DEPTH) ** 0.25
ATTN_SCALE = HEAD_DIM ** -0.5
NEG_INF = -1e30

LANES = 128
SUBLANES = 8
N_SLABS = N_HEADS * HEAD_DIM // LANES
FF_CHUNK = 256
VMEM_LIMIT = 56 * 1024 * 1024


def _params(n_axes):
    return pltpu.CompilerParams(dimension_semantics=("arbitrary",) * n_axes,
                                vmem_limit_bytes=VMEM_LIMIT)


def _resident(shape):
    nd = len(shape)
    return pl.BlockSpec(shape, lambda *_: (0,) * nd, pipeline_mode=pl.Buffered(1))


def _ada_kernel(c_ref, w_ref, b_ref, o_ref):
    sc = jax.nn.silu(c_ref[...])
    o_ref[0] = jnp.dot(sc.astype(BF16), w_ref[0].astype(BF16),
                       preferred_element_type=F32) + b_ref[0]


def _ada(c_all, w, b, nb):
    L, D, N = w.shape
    M = c_all.shape[0]
    return pl.pallas_call(
        _ada_kernel,
        out_shape=jax.ShapeDtypeStruct((L, M, N), F32),
        grid=(L, N // nb),
        in_specs=[pl.BlockSpec((M, D), lambda l, n: (0, 0)),
                  pl.BlockSpec((1, D, nb), lambda l, n: (l, 0, n)),
                  pl.BlockSpec((1, 1, nb), lambda l, n: (l, 0, n))],
        out_specs=pl.BlockSpec((1, M, nb), lambda l, n: (l, 0, n)),
        compiler_params=_params(2),
        name="ada",
    )(c_all, w, b.reshape(L, 1, N))


def _proj_kernel(x_ref, sc_ref, sh_ref, w_ref, *out_refs, n_slabs, f32_cols):
    h = (x_ref[0] * (1.0 + sc_ref[0]) + sh_ref[0]).astype(BF16)
    y = jnp.dot(h, w_ref[...], preferred_element_type=F32)
    outs = list(out_refs)
    if n_slabs:
        o = outs.pop(0)
        for p in range(n_slabs):
            o[0, p] = y[:, p * LANES:(p + 1) * LANES].astype(BF16)
    if f32_cols is not None:
        outs.pop(0)[0] = y[:, f32_cols[0]:f32_cols[1]]


def _proj(x, sc, sh, w, *, tm, n_tiles, row_off=0, n_slabs=0, f32_cols=None):
    B, _, D = x.shape
    N = w.shape[1]
    rows = tm * n_tiles
    out_shape, out_specs = [], []
    if n_slabs:
        out_shape.append(jax.ShapeDtypeStruct((B, n_slabs, rows, LANES), BF16))
        out_specs.append(pl.BlockSpec((1, n_slabs, tm, LANES), lambda b, t: (b, 0, t, 0)))
    if f32_cols is not None:
        nc = f32_cols[1] - f32_cols[0]
        out_shape.append(jax.ShapeDtypeStruct((B, rows, nc), F32))
        out_specs.append(pl.BlockSpec((1, tm, nc), lambda b, t: (b, t, 0)))
    return pl.pallas_call(
        functools.partial(_proj_kernel, n_slabs=n_slabs, f32_cols=f32_cols),
        out_shape=out_shape,
        grid=(B, n_tiles),
        in_specs=[pl.BlockSpec((1, tm, D), lambda b, t: (b, t + row_off, 0)),
                  pl.BlockSpec((1, 1, D), lambda b, t: (b, 0, 0)),
                  pl.BlockSpec((1, 1, D), lambda b, t: (b, 0, 0)),
                  _resident((D, N))],
        out_specs=out_specs,
        compiler_params=_params(2),
        name="proj",
    )(x, sc, sh, w)


def _bias_kernel(tab_ref, idx_ref, neg_ref, o_ref, *, n_entries):
    h = pl.program_id(0)
    idx = idx_ref[...]
    rows, width = o_ref.shape[1], idx.shape[1]

    def pick(e, acc):
        return jnp.where(idx == e, tab_ref[e, h], acc)

    base = lax.fori_loop(0, n_entries, pick, jnp.zeros(idx.shape, F32))
    full = jnp.broadcast_to(base[0:1], (rows, width))
    shifted = pltpu.roll(full, 0, 1, stride=1, stride_axis=0)
    o_ref[0] = shifted[:, :o_ref.shape[2]] + neg_ref[...]


def _toeplitz_bias(table_t, idx_of_offset, neg_mask, back_rows):
    n_entries, H = table_t.shape
    R, KR = neg_mask.shape
    width = -(-(KR + R) // LANES) * LANES
    pos = np.arange(width)
    offs = np.where(pos < KR, pos, pos - width) - back_rows
    idx_row = jnp.broadcast_to(idx_of_offset(jnp.asarray(offs, jnp.int32))[None, :],
                               (SUBLANES, width)).astype(jnp.int32)
    return pl.pallas_call(
        functools.partial(_bias_kernel, n_entries=n_entries),
        out_shape=jax.ShapeDtypeStruct((H, R, KR), F32),
        grid=(H,),
        in_specs=[pl.BlockSpec(memory_space=pltpu.SMEM),
                  pl.BlockSpec((SUBLANES, width), lambda h: (0, 0)),
                  pl.BlockSpec((R, KR), lambda h: (0, 0))],
        out_specs=pl.BlockSpec((1, R, KR), lambda h: (h, 0, 0)),
        compiler_params=_params(1),
        name="bias",
    )(table_t, idx_row, jnp.asarray(neg_mask, F32))


def _t5_bucket(rel):
    nb = N_BUCKETS // 2
    max_exact = nb // 2
    n = jnp.abs(rel)
    large = max_exact + (jnp.log(jnp.maximum(n, 1).astype(F32) / max_exact)
                         / math.log(T5_MAX_DIST / max_exact) * (nb - max_exact)).astype(jnp.int32)
    large = jnp.minimum(large, nb - 1)
    return jnp.where(rel > 0, nb, 0) + jnp.where(n < max_exact, n, large)


def _relclip_index(rel):
    return jnp.clip(-rel, -REL_CLIP, REL_CLIP) + REL_CLIP


def _band_mask(qpos, kpos, n_back, check_nonneg):
    qc = qpos // CHUNK
    kc = kpos // CHUNK
    valid = (kc[None, :] <= qc[:, None]) & (kc[None, :] >= qc[:, None] - n_back)
    if check_nonneg:
        valid &= kpos[None, :] >= 0
    return np.where(valid, 0.0, NEG_INF).astype(np.float32)


def _attn_kernel(*refs, sb, pb, group, first_tile_mask, has_sink):
    refs = list(refs)
    sink_ref = refs.pop(0) if has_sink else None
    q_ref, kp_ref, kc_ref, vp_ref, vc_ref, bias_ref, o_ref = refs
    tq = q_ref.shape[2]
    kr = sb + pb
    n_sub = tq // sb
    pairs_per_unit = 2 if group == 4 else 1
    heads_per_unit = 2 * pairs_per_unit
    n_units = N_SLABS // pairs_per_unit
    tile = pl.program_id(1)
    lane = lax.broadcasted_iota(jnp.int32, (1, LANES), 1)
    low_half = lane < HEAD_DIM
    row_head = lax.broadcasted_iota(jnp.int32, (heads_per_unit * sb, 1), 0) // sb

    def unit(u, carry):
        if group == 4:
            kv_slab, kv_half = u // 2, u % 2
        else:
            kv_slab = u
        k_all = jnp.concatenate([kp_ref[0, kv_slab].astype(BF16), kc_ref[0, kv_slab].astype(BF16)], 0)
        v_all = jnp.concatenate([vp_ref[0, kv_slab].astype(BF16), vc_ref[0, kv_slab].astype(BF16)], 0)
        if group == 4:
            own = (lane >= HEAD_DIM * kv_half) & (lane < HEAD_DIM * (kv_half + 1))
            k_all = jnp.where(own, k_all, pltpu.roll(k_all, HEAD_DIM, 1))
            v_all = jnp.where(own, v_all, pltpu.roll(v_all, HEAD_DIM, 1))
        if has_sink:
            sink = jnp.zeros((heads_per_unit * sb, 1), F32)
            for hh in range(heads_per_unit):
                sink = jnp.where(row_head == hh, sink_ref[u * heads_per_unit + hh], sink)
        for i in range(n_sub):
            r0 = i * sb
            kk = k_all[r0:r0 + kr]
            vv = v_all[r0:r0 + kr]
            q_rows = []
            for pi in range(pairs_per_unit):
                qp = q_ref[0, u * pairs_per_unit + pi, r0:r0 + sb, :]
                q_rows += [jnp.where(low_half, qp, jnp.zeros_like(qp)),
                           jnp.where(low_half, jnp.zeros_like(qp), qp)]
            qs = jnp.concatenate(q_rows, 0)
            s = lax.dot_general(qs, kk, (((1,), (1,)), ((), ())), preferred_element_type=F32)
            s = s + bias_ref[u]
            if first_tile_mask and r0 < pb:
                col = lax.broadcasted_iota(jnp.int32, (1, kr), 1)
                s = s + jnp.where((col + r0 < pb) & (tile == 0), NEG_INF, 0.0)
            m = jnp.max(s, axis=-1, keepdims=True)
            if has_sink:
                m = jnp.maximum(m, sink)
            e = jnp.exp(s - m)
            denom = jnp.sum(e, axis=-1, keepdims=True)
            if has_sink:
                denom = denom + jnp.exp(sink - m)
            pv = jnp.dot(e.astype(BF16), vv, preferred_element_type=F32)
            pv = pv * (1.0 / denom)
            for pi in range(pairs_per_unit):
                lo = pv[(2 * pi) * sb:(2 * pi + 1) * sb]
                hi = pv[(2 * pi + 1) * sb:(2 * pi + 2) * sb]
                o_ref[0, u * pairs_per_unit + pi, r0:r0 + sb, :] = jnp.where(low_half, lo, hi).astype(BF16)
        return carry

    lax.fori_loop(0, n_units, unit, 0)


def _attention(q_arr, q_blk, k_arr, k_blk, v_arr, v_blk, kp_arr, vp_arr, prev_index, bias, sinks,
               *, T, tq, sb, pb, group, first_tile_mask):
    B = q_arr.shape[0]
    kv_slabs = N_SLABS if group == 1 else N_KV_A * HEAD_DIM // LANES
    has_sink = sinks is not None
    in_specs = [pl.BlockSpec(memory_space=pltpu.SMEM)] if has_sink else []
    in_specs += [
        pl.BlockSpec((1, N_SLABS, tq, LANES), lambda b, t: (b, q_blk, t, 0)),
        pl.BlockSpec((1, kv_slabs, pb, LANES), lambda b, t: prev_index(b, t, k_blk)),
        pl.BlockSpec((1, kv_slabs, tq, LANES), lambda b, t: (b, k_blk, t, 0)),
        pl.BlockSpec((1, kv_slabs, pb, LANES), lambda b, t: prev_index(b, t, v_blk)),
        pl.BlockSpec((1, kv_slabs, tq, LANES), lambda b, t: (b, v_blk, t, 0)),
        _resident(bias.shape),
    ]
    args = ([sinks] if has_sink else []) + [q_arr, kp_arr, k_arr, vp_arr, v_arr, bias]
    return pl.pallas_call(
        functools.partial(_attn_kernel, sb=sb, pb=pb, group=group,
                          first_tile_mask=first_tile_mask, has_sink=has_sink),
        out_shape=jax.ShapeDtypeStruct((B, N_SLABS, T, LANES), BF16),
        grid=(B, T // tq),
        in_specs=in_specs,
        out_specs=pl.BlockSpec((1, N_SLABS, tq, LANES), lambda b, t: (b, 0, t, 0)),
        compiler_params=_params(2),
        name="attn",
    )(*args)


def _layer_norm(y, g, b):
    mu = jnp.mean(y, axis=-1, keepdims=True)
    d = y - mu
    var = jnp.mean(d * d, axis=-1, keepdims=True)
    return d * lax.rsqrt(var + LN_EPS) * g + b


def _gelu_tanh(x):
    return 0.5 * x * (1.0 + jnp.tanh(math.sqrt(2.0 / math.pi) * (x + 0.044715 * (x * x * x))))


def _causal_conv(u, tail, cw_ref, cb_ref, cols):
    row = lax.broadcasted_iota(jnp.int32, (SUBLANES, 1), 0)
    back1 = pltpu.roll(u, 1, 0)
    back2 = pltpu.roll(u, 2, 0)
    head1 = jnp.where(row < 1, pltpu.roll(tail, 1, 0), back1[:SUBLANES])
    head2 = jnp.where(row < 2, pltpu.roll(tail, 2, 0), back2[:SUBLANES])
    back1 = jnp.concatenate([head1, back1[SUBLANES:]], 0)
    back2 = jnp.concatenate([head2, back2[SUBLANES:]], 0)
    y = cb_ref[:, cols] + cw_ref[0:1, cols] * back2
    y = y + cw_ref[1:2, cols] * back1
    return y + cw_ref[2:3, cols] * u


def _post_kernel(x_ref, o_ref, wo_ref, gm_ref, scf_ref, shf_ref, gf_ref, lng_ref, lnb_ref,
                 wup_ref, cw_ref, cb_ref, wdn_ref, prev_ref, xo_ref, tail_ref, act_ref):
    tm = x_ref.shape[1]

    @pl.when(pl.program_id(1) == 0)
    def _():
        tail_ref[0] = prev_ref[0]

    attn = jnp.concatenate([o_ref[0, p] for p in range(N_SLABS)], axis=-1)
    mix = jnp.dot(attn, wo_ref[...], preferred_element_type=F32)
    x1 = _layer_norm(DEEPNORM_ALPHA * x_ref[0] + (1.0 + gm_ref[0]) * mix, lng_ref[0:1], lnb_ref[0:1])
    h = (x1 * (1.0 + scf_ref[0]) + shf_ref[0]).astype(BF16)
    for c in range(D_FF // FF_CHUNK):
        cols_a = slice(c * FF_CHUNK, (c + 1) * FF_CHUNK)
        cols_g = slice(D_FF + c * FF_CHUNK, D_FF + (c + 1) * FF_CHUNK)
        ua = jnp.dot(h, wup_ref[:, cols_a], preferred_element_type=F32)
        ug = jnp.dot(h, wup_ref[:, cols_g], preferred_element_type=F32)
        ya = _causal_conv(ua, tail_ref[0, :, cols_a], cw_ref, cb_ref, cols_a)
        yg = _causal_conv(ug, tail_ref[0, :, cols_g], cw_ref, cb_ref, cols_g)
        tail_ref[0, :, cols_a] = ua[tm - SUBLANES:]
        tail_ref[0, :, cols_g] = ug[tm - SUBLANES:]
        act_ref[:, cols_a] = (_gelu_tanh(yg) * ya).astype(BF16)
    f = jnp.dot(act_ref[...], wdn_ref[...], preferred_element_type=F32)
    xo_ref[0] = _layer_norm(DEEPNORM_ALPHA * x1 + (1.0 + gf_ref[0]) * f, lng_ref[1:2], lnb_ref[1:2])


def _post(x, o, wo, gm, scf, shf, gf, lng, lnb, wup, cw, cb, wdn, prev8, *, tm):
    B, T, D = x.shape
    vec = pl.BlockSpec((1, 1, D), lambda b, t: (b, 0, 0))
    return pl.pallas_call(
        _post_kernel,
        out_shape=[jax.ShapeDtypeStruct((B, T, D), F32),
                   jax.ShapeDtypeStruct((B, SUBLANES, 2 * D_FF), F32)],
        grid=(B, T // tm),
        in_specs=[pl.BlockSpec((1, tm, D), lambda b, t: (b, t, 0)),
                  pl.BlockSpec((1, N_SLABS, tm, LANES), lambda b, t: (b, 0, t, 0)),
                  _resident(wo.shape), vec, vec, vec, vec,
                  _resident(lng.shape), _resident(lnb.shape),
                  _resident(wup.shape), _resident(cw.shape), _resident(cb.shape), _resident(wdn.shape),
                  pl.BlockSpec((1, SUBLANES, 2 * D_FF), lambda b, t: (b, 0, 0))],
        out_specs=[pl.BlockSpec((1, tm, D), lambda b, t: (b, t, 0)),
                   pl.BlockSpec((1, SUBLANES, 2 * D_FF), lambda b, t: (b, 0, 0))],
        scratch_shapes=[pltpu.VMEM((tm, D_FF), BF16)],
        compiler_params=_params(2),
        name="post",
    )(x, o, wo, gm, scf, shf, gf, lng, lnb, wup, cw, cb, wdn, prev8)


def _run_group(x, ada, ada_kv, caches, wts, biases, *, prompt):
    B, T, D = x.shape
    tm = min(T, 512)
    n_tiles = T // tm
    cache_a_k, cache_a_v, cache_b_k, cache_b_v, state_conv = caches
    new_ak, new_av, new_conv = [], [], []
    kv_b = b_k_rows = b_v_rows = None
    qa, ka = N_HEADS * HEAD_DIM, N_KV_A * HEAD_DIM

    def slabs(cache):
        Bc, P = cache.shape[:2]
        return jnp.swapaxes(cache.reshape(Bc, P, -1, LANES), 1, 2)

    def prev_block(ratio):
        return lambda b, t, blk: (b, blk, jnp.maximum(t * ratio - 1, 0), 0)

    cache_block = lambda b, t, blk: (b, 0, 0, 0)

    for l in range(DEPTH):
        vecs = [ada[l][:, i * D:(i + 1) * D].reshape(B, 1, D) for i in range(6)]
        sh_m, sc_m, g_m, sh_f, sc_f, g_f = vecs
        if l < N_A_LAYERS:
            pb = A_BACK * CHUNK
            if prompt:
                qkv, = _proj(x, sc_m, sh_m, wts["qkv"][l], tm=tm, n_tiles=n_tiles, n_slabs=12)
                rows = min(pb, T)
                kv_tail, = _proj(x, sc_m, sh_m, wts["qkv"][l][:, qa:], tm=rows, n_tiles=1,
                                 row_off=T // rows - 1, f32_cols=(0, 2 * ka))
                o = _attention(qkv, 0, qkv, 4, qkv, 5, qkv, qkv, prev_block(tm // pb),
                               biases["a"], wts["sinks"][l],
                               T=T, tq=tm, sb=2 * CHUNK, pb=pb, group=4, first_tile_mask=True)
            else:
                qkv, kv_tail = _proj(x, sc_m, sh_m, wts["qkv"][l], tm=tm, n_tiles=n_tiles, n_slabs=12,
                                     f32_cols=(qa, qa + 2 * ka))
                o = _attention(qkv, 0, qkv, 4, qkv, 5, slabs(cache_a_k[l]), slabs(cache_a_v[l]),
                               cache_block, biases["a"], wts["sinks"][l],
                               T=T, tq=tm, sb=tm, pb=cache_a_k.shape[2], group=4, first_tile_mask=False)
            new_ak.append(kv_tail[..., :ka].reshape(B, -1, N_KV_A, HEAD_DIM))
            new_av.append(kv_tail[..., ka:].reshape(B, -1, N_KV_A, HEAD_DIM))
            wo = wts["o_a"][l]
        else:
            j = l - N_A_LAYERS
            pb = B_BACK * CHUNK
            q, = _proj(x, sc_m, sh_m, wts["q_b"][j], tm=tm, n_tiles=n_tiles, n_slabs=N_SLABS)
            if prompt:
                o = _attention(q, 0, kv_b, 0, kv_b, 1, kv_b, kv_b, prev_block(tm // pb),
                               biases["b"][j], None,
                               T=T, tq=tm, sb=4 * CHUNK, pb=pb, group=1, first_tile_mask=True)
            else:
                o = _attention(q, 0, kv_b, 0, kv_b, 1, slabs(cache_b_k), slabs(cache_b_v),
                               cache_block, biases["b"][j], None,
                               T=T, tq=tm, sb=tm, pb=cache_b_k.shape[1], group=1, first_tile_mask=False)
            wo = wts["o_b"][j]
        if prompt:
            prev8 = jnp.zeros((B, SUBLANES, 2 * D_FF), F32)
        else:
            prev8 = jnp.pad(state_conv[l], ((0, 0), (SUBLANES - (CONV_W - 1), 0), (0, 0)))
        x, tail8 = _post(x, o, wo, g_m, sc_f, sh_f, g_f, wts["ln_g"][l], wts["ln_b"][l],
                         wts["up"][l], wts["conv_w"][l], wts["conv_b"][l], wts["down"][l], prev8, tm=tm)
        new_conv.append(tail8[:, SUBLANES - (CONV_W - 1):])
        if l == N_A_LAYERS - 1:
            sh_kv = ada_kv[:, :D].reshape(B, 1, D)
            sc_kv = ada_kv[:, D:].reshape(B, 1, D)
            nkv = 2 * N_HEADS * HEAD_DIM
            if prompt:
                kv_b, = _proj(x, sc_kv, sh_kv, wts["kv_b"], tm=tm, n_tiles=n_tiles, n_slabs=2 * N_SLABS)
                rows = min(B_BACK * CHUNK, T)
                kv_rows, = _proj(x, sc_kv, sh_kv, wts["kv_b"], tm=rows, n_tiles=1,
                                 row_off=T // rows - 1, f32_cols=(0, nkv))
            else:
                kv_b, kv_rows = _proj(x, sc_kv, sh_kv, wts["kv_b"], tm=tm, n_tiles=n_tiles,
                                      n_slabs=2 * N_SLABS, f32_cols=(0, nkv))
            b_k_rows = kv_rows[..., :nkv // 2].reshape(B, -1, N_HEADS, HEAD_DIM)
            b_v_rows = kv_rows[..., nkv // 2:].reshape(B, -1, N_HEADS, HEAD_DIM)
    return x, jnp.stack(new_ak), jnp.stack(new_av), b_k_rows, b_v_rows, jnp.stack(new_conv)


def _group_biases(t5_table, relpos_b, *, prompt, T, rows_a, rows_b):
    def build(table_t, index_fn, n_back, sb, pb, heads_per_unit):
        if prompt:
            qpos = np.arange(sb)
            kpos = np.arange(sb + pb) - pb
            neg = _band_mask(qpos + pb, kpos + pb, n_back, False)
        else:
            qpos = PAST_LEN + np.arange(sb)
            kpos = PAST_LEN - pb + np.arange(sb + pb)
            neg = _band_mask(qpos, kpos, n_back, True)
        bias = _toeplitz_bias(table_t, index_fn, neg, pb)
        return bias.reshape(N_HEADS // heads_per_unit, heads_per_unit * sb, sb + pb)

    if prompt:
        a = build(t5_table, _t5_bucket, A_BACK, 2 * CHUNK, A_BACK * CHUNK, 4)
        b = [build(relpos_b[j].T, _relclip_index, B_BACK, 4 * CHUNK, B_BACK * CHUNK, 2)
             for j in range(relpos_b.shape[0])]
    else:
        a = build(t5_table, _t5_bucket, A_BACK, T, rows_a, 4)
        b = [build(relpos_b[j].T, _relclip_index, B_BACK, T, rows_b, 2)
             for j in range(relpos_b.shape[0])]
    return {"a": a, "b": b}


def kernel(x_prompt, x_sample, c_prompt, c_sample, cache_a_k, cache_a_v, cache_b_k, cache_b_v, state_conv, w_ada, b_ada, ln_g, ln_b, w_qkv_a, w_o_a, sinks_a, t5_table, w_ada_kv, b_ada_kv, w_kv_b, w_q_b, w_o_b, relpos_b, w_up, conv_w, conv_b, w_down):
    B = x_prompt.shape[0]
    qa = N_HEADS * HEAD_DIM
    qscale = jnp.concatenate([jnp.full((qa,), ATTN_SCALE, F32),
                              jnp.ones((w_qkv_a.shape[2] - qa,), F32)])
    wts = {
        "qkv": (w_qkv_a * qscale).astype(BF16),
        "o_a": w_o_a.astype(BF16),
        "sinks": sinks_a,
        "kv_b": w_kv_b.astype(BF16),
        "q_b": (w_q_b * ATTN_SCALE).astype(BF16),
        "o_b": w_o_b.astype(BF16),
        "up": w_up.astype(BF16),
        "conv_w": conv_w,
        "conv_b": conv_b.reshape(DEPTH, 1, 2 * D_FF),
        "down": w_down.astype(BF16),
        "ln_g": ln_g,
        "ln_b": ln_b,
    }
    c_all = jnp.concatenate([c_prompt, c_sample], axis=0)
    ada = _ada(c_all, w_ada, b_ada, 1536)
    ada_kv = _ada(c_all, w_ada_kv[None], b_ada_kv[None], 1024)[0]

    bias_p = _group_biases(t5_table, relpos_b, prompt=True, T=x_prompt.shape[1], rows_a=0, rows_b=0)
    bias_s = _group_biases(t5_table, relpos_b, prompt=False, T=x_sample.shape[1],
                           rows_a=cache_a_k.shape[2], rows_b=cache_b_k.shape[1])

    y_p, p_ak, p_av, p_bk, p_bv, p_conv = _run_group(
        x_prompt, ada[:, :B], ada_kv[:B], (None,) * 5, wts, bias_p, prompt=True)
    y_s, s_ak, s_av, s_bk, s_bv, s_conv = _run_group(
        x_sample, ada[:, B:], ada_kv[B:], (cache_a_k, cache_a_v, cache_b_k, cache_b_v, state_conv),
        wts, bias_s, prompt=False)
    return (y_p, y_s, p_ak, p_av, p_bk, p_bv, p_conv, s_ak, s_av, s_bk, s_bv, s_conv)
```

```python
import functools
import math

import numpy as np
import jax
import jax.numpy as jnp
from jax import lax
from jax.experimental import pallas as pl
from jax.experimental.pallas import tpu as pltpu

F32 = jnp.float32
BF16 = jnp.bfloat16

D_MODEL = 1024
DEPTH = 4
PAST_LEN = 2048
CHUNK = 64
HEAD_DIM = 64
N_A_LAYERS = DEPTH // 2
N_HEADS = 16
N_KV_A = 4
A_BACK = 2
B_BACK = 8
REL_CLIP = 128
N_BUCKETS = 32
T5_MAX_DIST = 128
D_FF = 2816
CONV_W = 3
LN_EPS = 1e-5
DEEPNORM_ALPHA = (2.0 * DEPTH) ** 0.25
ATTN_SCALE = HEAD_DIM ** -0.5
NEG_INF = -1e30

LANES = 128
SUBLANES = 8
N_SLABS = N_HEADS * HEAD_DIM // LANES
KV_SLABS_A = N_KV_A * HEAD_DIM // LANES
FF_CHUNK = 256
VMEM_LIMIT = 56 * 1024 * 1024


def _params(n_axes):
    return pltpu.CompilerParams(dimension_semantics=("arbitrary",) * n_axes,
                                vmem_limit_bytes=VMEM_LIMIT)


def _resident(shape):
    nd = len(shape)
    return pl.BlockSpec(shape, lambda *_: (0,) * nd, pipeline_mode=pl.Buffered(1))


def _ada_kernel(c_ref, w_ref, b_ref, o_ref):
    sc = jax.nn.silu(c_ref[...])
    o_ref[0] = jnp.dot(sc.astype(BF16), w_ref[0].astype(BF16),
                       preferred_element_type=F32) + b_ref[0]


def _ada(c_all, w, b, nb):
    L, D, N = w.shape
    M = c_all.shape[0]
    return pl.pallas_call(
        _ada_kernel,
        out_shape=jax.ShapeDtypeStruct((L, M, N), F32),
        grid=(L, N // nb),
        in_specs=[pl.BlockSpec((M, D), lambda l, n: (0, 0)),
                  pl.BlockSpec((1, D, nb), lambda l, n: (l, 0, n)),
                  pl.BlockSpec((1, 1, nb), lambda l, n: (l, 0, n))],
        out_specs=pl.BlockSpec((1, M, nb), lambda l, n: (l, 0, n)),
        compiler_params=_params(2),
        name="ada",
    )(c_all, w, b.reshape(L, 1, N))


def _proj_kernel(x_ref, sc_ref, sh_ref, w_ref, *out_refs, groups, f32_cols):
    h = (x_ref[0] * (1.0 + sc_ref[0]) + sh_ref[0]).astype(BF16)
    y = jnp.dot(h, w_ref[...], preferred_element_type=F32)
    outs = list(out_refs)
    for col0, n_slabs, transposed in groups:
        o = outs.pop(0)
        for p in range(n_slabs):
            blk = y[:, col0 + p * LANES:col0 + (p + 1) * LANES]
            o[0, p] = (blk.T if transposed else blk).astype(BF16)
    if f32_cols is not None:
        outs.pop(0)[0] = y[:, f32_cols[0]:f32_cols[1]]


def _proj(x, sc, sh, w, *, tm, n_tiles, row_off=0, groups=(), f32_cols=None):
    B, _, D = x.shape
    N = w.shape[1]
    rows = tm * n_tiles
    out_shape, out_specs = [], []
    for _, n_slabs, transposed in groups:
        if transposed:
            out_shape.append(jax.ShapeDtypeStruct((B, n_slabs, LANES, rows), BF16))
            out_specs.append(pl.BlockSpec((1, n_slabs, LANES, tm), lambda b, t: (b, 0, 0, t)))
        else:
            out_shape.append(jax.ShapeDtypeStruct((B, n_slabs, rows, LANES), BF16))
            out_specs.append(pl.BlockSpec((1, n_slabs, tm, LANES), lambda b, t: (b, 0, t, 0)))
    if f32_cols is not None:
        nc = f32_cols[1] - f32_cols[0]
        out_shape.append(jax.ShapeDtypeStruct((B, rows, nc), F32))
        out_specs.append(pl.BlockSpec((1, tm, nc), lambda b, t: (b, t, 0)))
    return pl.pallas_call(
        functools.partial(_proj_kernel, groups=tuple(groups), f32_cols=f32_cols),
        out_shape=out_shape,
        grid=(B, n_tiles),
        in_specs=[pl.BlockSpec((1, tm, D), lambda b, t: (b, t + row_off, 0)),
                  pl.BlockSpec((1, 1, D), lambda b, t: (b, 0, 0)),
                  pl.BlockSpec((1, 1, D), lambda b, t: (b, 0, 0)),
                  _resident((D, N))],
        out_specs=out_specs,
        compiler_params=_params(2),
        name="proj",
    )(x, sc, sh, w)


def _bias_kernel(tab_ref, idx_ref, neg_ref, o_ref, *, n_entries):
    h = pl.program_id(0)
    idx = idx_ref[...]
    n_keys, n_q = o_ref.shape[1], o_ref.shape[2]

    def pick(e, acc):
        return jnp.where(idx == e, tab_ref[e, h], acc)

    base = lax.fori_loop(0, n_entries, pick, jnp.zeros(idx.shape, F32))
    full = jnp.broadcast_to(base[0:1], (n_keys, idx.shape[1]))
    shifted = pltpu.roll(full, 0, 1, stride=1, stride_axis=0)
    o_ref[0] = shifted[:, :n_q] + neg_ref[...]


def _toeplitz_bias(table_t, idx_of_offset, neg_mask_t, back_rows, heads_per_unit):
    n_entries, H = table_t.shape
    KR, R = neg_mask_t.shape
    width = -(-(KR + R) // LANES) * LANES
    pos = np.arange(width)
    q_minus_key_row = np.where(pos < R, pos, pos - width)
    offs = -q_minus_key_row - back_rows
    idx_row = jnp.broadcast_to(idx_of_offset(jnp.asarray(offs, jnp.int32))[None, :],
                               (SUBLANES, width)).astype(jnp.int32)
    return pl.pallas_call(
        functools.partial(_bias_kernel, n_entries=n_entries),
        out_shape=jax.ShapeDtypeStruct((H // heads_per_unit, KR, heads_per_unit * R), F32),
        grid=(H,),
        in_specs=[pl.BlockSpec(memory_space=pltpu.SMEM),
                  pl.BlockSpec((SUBLANES, width), lambda h: (0, 0)),
                  pl.BlockSpec((KR, R), lambda h: (0, 0))],
        out_specs=pl.BlockSpec((1, KR, R), lambda h: (h // heads_per_unit, 0, h % heads_per_unit)),
        compiler_params=_params(1),
        name="bias",
    )(table_t, idx_row, jnp.asarray(neg_mask_t, F32))


def _t5_bucket(rel):
    nb = N_BUCKETS // 2
    max_exact = nb // 2
    n = jnp.abs(rel)
    large = max_exact + (jnp.log(jnp.maximum(n, 1).astype(F32) / max_exact)
                         / math.log(T5_MAX_DIST / max_exact) * (nb - max_exact)).astype(jnp.int32)
    large = jnp.minimum(large, nb - 1)
    return jnp.where(rel > 0, nb, 0) + jnp.where(n < max_exact, n, large)


def _relclip_index(rel):
    return jnp.clip(-rel, -REL_CLIP, REL_CLIP) + REL_CLIP


def _band_valid(qpos, kpos, n_back):
    qc = qpos // CHUNK
    kc = kpos // CHUNK
    return (kc[None, :] <= qc[:, None]) & (kc[None, :] >= qc[:, None] - n_back)


def _attn_kernel(*refs, sb, pb, group, first_tile_mask, has_sink):
    refs = list(refs)
    sink_ref = refs.pop(0) if has_sink else None
    q_ref, kp_ref, kc_ref, vp_ref, vc_ref, bias_ref, o_ref, s_ref, e_ref, m_ref, inv_ref = refs
    tq = q_ref.shape[3]
    kr = sb + pb
    n_sub = tq // sb
    pairs_per_unit = 2 if group == 4 else 1
    heads_per_unit = 2 * pairs_per_unit
    cols = heads_per_unit * sb
    n_units = N_SLABS // pairs_per_unit
    units_per_trip = 1 if n_sub % 2 == 0 else 2
    items_per_trip = units_per_trip * n_sub
    tile = pl.program_id(1)
    lane = lax.broadcasted_iota(jnp.int32, (1, LANES), 1)
    dim_row = lax.broadcasted_iota(jnp.int32, (LANES, 1), 0)
    low_dims = dim_row < HEAD_DIM
    col_head = lax.broadcasted_iota(jnp.int32, (1, cols), 1) // sb

    def kv_slab(u):
        return u // 2 if group == 4 else u

    def key_rows(u, i):
        r0 = i * sb
        parts = []
        if r0 < pb:
            parts.append(kp_ref[0, kv_slab(u), r0:pb, :].astype(BF16))
        parts.append(kc_ref[0, kv_slab(u), max(r0 - pb, 0):r0 + kr - pb, :].astype(BF16))
        x = parts[0] if len(parts) == 1 else jnp.concatenate(parts, 0)
        if group == 4:
            half = u % 2
            own = (lane >= HEAD_DIM * half) & (lane < HEAD_DIM * (half + 1))
            x = jnp.where(own, x, pltpu.roll(x, HEAD_DIM, 1))
        return x

    def value_cols(u, i):
        r0 = i * sb
        parts = []
        if r0 < pb:
            parts.append(vp_ref[0, kv_slab(u), :, r0:pb].astype(BF16))
        parts.append(vc_ref[0, kv_slab(u), :, max(r0 - pb, 0):r0 + kr - pb].astype(BF16))
        x = parts[0] if len(parts) == 1 else jnp.concatenate(parts, 1)
        if group == 4:
            half = u % 2
            own = (dim_row >= HEAD_DIM * half) & (dim_row < HEAD_DIM * (half + 1))
            x = jnp.where(own, x, jnp.concatenate([x[HEAD_DIM:], x[:HEAD_DIM]], 0))
        return x

    def sink_row(u):
        row = jnp.zeros((1, cols), F32)
        for hh in range(heads_per_unit):
            row = jnp.where(col_head == hh, sink_ref[u * heads_per_unit + hh], row)
        return row

    def scores(u, i, slot, masked):
        q_cols = []
        for pi in range(pairs_per_unit):
            qp = q_ref[0, u * pairs_per_unit + pi, :, i * sb:(i + 1) * sb]
            q_cols += [jnp.where(low_dims, qp, jnp.zeros_like(qp)),
                       jnp.where(low_dims, jnp.zeros_like(qp), qp)]
        qs = jnp.concatenate(q_cols, 1)
        s = jnp.dot(key_rows(u, i), qs, preferred_element_type=F32)
        n_before_start = pb - i * sb if masked else 0
        for c0 in range(0, cols, LANES):
            sc = s[:, c0:c0 + LANES] + bias_ref[u, :, c0:c0 + LANES]
            if n_before_start > 0:
                sc = jnp.concatenate([sc[:n_before_start] + NEG_INF, sc[n_before_start:]], 0)
            m = jnp.max(sc, axis=0, keepdims=True)
            if has_sink:
                m = jnp.maximum(m, sink_row(u)[:, c0:c0 + LANES])
            s_ref[slot, :, c0:c0 + LANES] = sc
            m_ref[slot, :, c0:c0 + LANES] = jnp.broadcast_to(m, (SUBLANES, LANES))

    def softmax(u, i, slot):
        for c0 in range(0, cols, LANES):
            m = m_ref[slot, 0:1, c0:c0 + LANES]
            e = jnp.exp(s_ref[slot, :, c0:c0 + LANES] - m)
            denom = jnp.sum(e, axis=0, keepdims=True)
            if has_sink:
                denom = denom + jnp.exp(sink_row(u)[:, c0:c0 + LANES] - m)
            e_ref[slot, :, c0:c0 + LANES] = e.astype(BF16)
            inv_ref[slot, :, c0:c0 + LANES] = jnp.broadcast_to(1.0 / denom, (SUBLANES, LANES))

    def values(u, i, slot):
        o = jnp.dot(value_cols(u, i), e_ref[slot], preferred_element_type=F32)
        o = o * inv_ref[slot, 0:1, :]
        for pi in range(pairs_per_unit):
            lo = o[:, (2 * pi) * sb:(2 * pi + 1) * sb]
            hi = o[:, (2 * pi + 1) * sb:(2 * pi + 2) * sb]
            o_ref[0, u * pairs_per_unit + pi, :, i * sb:(i + 1) * sb] = jnp.where(low_dims, lo, hi).astype(BF16)

    def run(masked):
        scores(0, 0, 0, masked)
        e_ref[1] = jnp.zeros(e_ref.shape[1:], BF16)
        inv_ref[1] = jnp.ones(inv_ref.shape[1:], F32)

        def trip(t, carry):
            for j in range(items_per_trip):
                u = t * units_per_trip + j // n_sub
                i = j % n_sub
                if j + 1 < items_per_trip:
                    nxt = (t * units_per_trip + (j + 1) // n_sub, (j + 1) % n_sub)
                else:
                    nxt = (jnp.minimum((t + 1) * units_per_trip, n_units - 1), 0)
                if j >= 1:
                    prv = (t * units_per_trip + (j - 1) // n_sub, (j - 1) % n_sub)
                else:
                    prv = (jnp.maximum(t * units_per_trip - 1, 0), n_sub - 1)
                scores(*nxt, (j + 1) % 2, masked)
                softmax(u, i, j % 2)
                values(*prv, (j + 1) % 2)
            return carry

        lax.fori_loop(0, n_units // units_per_trip, trip, 0)
        values(n_units - 1, n_sub - 1, (items_per_trip - 1) % 2)

    if first_tile_mask:
        pl.when(tile == 0)(lambda: run(True))
        pl.when(tile != 0)(lambda: run(False))
    else:
        run(False)


def _attention(q_t, k, v_t, k_prev, v_t_prev, prev_block, bias, sinks, *, tq, sb, group, first_tile_mask):
    B, _, _, T = q_t.shape
    kv_slabs = k.shape[1]
    pb = bias.shape[1] - sb
    cols = (4 if group == 4 else 2) * sb
    has_sink = sinks is not None
    in_specs = [pl.BlockSpec(memory_space=pltpu.SMEM)] if has_sink else []
    in_specs += [
        pl.BlockSpec((1, N_SLABS, LANES, tq), lambda b, t: (b, 0, 0, t)),
        pl.BlockSpec((1, kv_slabs, pb, LANES), lambda b, t: (b, 0, prev_block(t), 0)),
        pl.BlockSpec((1, kv_slabs, tq, LANES), lambda b, t: (b, 0, t, 0)),
        pl.BlockSpec((1, kv_slabs, LANES, pb), lambda b, t: (b, 0, 0, prev_block(t))),
        pl.BlockSpec((1, kv_slabs, LANES, tq), lambda b, t: (b, 0, 0, t)),
        _resident(bias.shape),
    ]
    args = ([sinks] if has_sink else []) + [q_t, k_prev, k, v_t_prev, v_t, bias]
    return pl.pallas_call(
        functools.partial(_attn_kernel, sb=sb, pb=pb, group=group,
                          first_tile_mask=first_tile_mask, has_sink=has_sink),
        out_shape=jax.ShapeDtypeStruct((B, N_SLABS, LANES, T), BF16),
        grid=(B, T // tq),
        in_specs=in_specs,
        out_specs=pl.BlockSpec((1, N_SLABS, LANES, tq), lambda b, t: (b, 0, 0, t)),
        scratch_shapes=[pltpu.VMEM((2, sb + pb, cols), F32),
                        pltpu.VMEM((2, sb + pb, cols), BF16),
                        pltpu.VMEM((2, SUBLANES, cols), F32),
                        pltpu.VMEM((2, SUBLANES, cols), F32)],
        compiler_params=_params(2),
        name="attn",
    )(*args)


def _layer_norm(y, g, b):
    mu = jnp.mean(y, axis=-1, keepdims=True)
    d = y - mu
    var = jnp.mean(d * d, axis=-1, keepdims=True)
    return d * lax.rsqrt(var + LN_EPS) * g + b


def _gelu_tanh(x):
    return 0.5 * x * (1.0 + jnp.tanh(math.sqrt(2.0 / math.pi) * (x + 0.044715 * (x * x * x))))


def _causal_conv(u, tail, buf_ref, cw_ref, cb_ref, cols):
    tm, n = u.shape
    parts = []
    for s in range(n // LANES):
        lanes = slice(s * LANES, (s + 1) * LANES)
        buf_ref[s, pl.ds(0, SUBLANES, stride=2), :] = tail[:, lanes]
        buf_ref[s, pl.ds(2 * SUBLANES, tm, stride=2), :] = u[:, lanes]
    for s in range(n // LANES):
        lanes = slice(cols.start + s * LANES, cols.start + (s + 1) * LANES)
        back1 = buf_ref[s, pl.ds(2 * SUBLANES - 2, tm, stride=2), :]
        back2 = buf_ref[s, pl.ds(2 * SUBLANES - 4, tm, stride=2), :]
        y = cb_ref[:, lanes] + cw_ref[0:1, lanes] * back2
        y = y + cw_ref[1:2, lanes] * back1
        parts.append(y + cw_ref[2:3, lanes] * u[:, s * LANES:(s + 1) * LANES])
    return jnp.concatenate(parts, axis=-1)


def _post_kernel(x_ref, o_ref, wo_ref, gm_ref, scf_ref, shf_ref, gf_ref, lng_ref, lnb_ref,
                 wup_ref, cw_ref, cb_ref, wdn_ref, prev_ref, xo_ref, tail_ref, act_ref, buf_ref,
                 *, o_transposed):
    tm = x_ref.shape[1]

    @pl.when(pl.program_id(1) == 0)
    def _():
        tail_ref[0] = prev_ref[0]

    if o_transposed:
        attn_t = jnp.concatenate([o_ref[0, p] for p in range(N_SLABS)], axis=0)
        mix = lax.dot_general(attn_t, wo_ref[...], (((0,), (0,)), ((), ())), preferred_element_type=F32)
    else:
        attn = jnp.concatenate([o_ref[0, p] for p in range(N_SLABS)], axis=-1)
        mix = jnp.dot(attn, wo_ref[...], preferred_element_type=F32)
    x1 = _layer_norm(DEEPNORM_ALPHA * x_ref[0] + (1.0 + gm_ref[0]) * mix, lng_ref[0:1], lnb_ref[0:1])
    h = (x1 * (1.0 + scf_ref[0]) + shf_ref[0]).astype(BF16)
    for c in range(D_FF // FF_CHUNK):
        cols_a = slice(c * FF_CHUNK, (c + 1) * FF_CHUNK)
        cols_g = slice(D_FF + c * FF_CHUNK, D_FF + (c + 1) * FF_CHUNK)
        ua = jnp.dot(h, wup_ref[:, cols_a], preferred_element_type=F32)
        ug = jnp.dot(h, wup_ref[:, cols_g], preferred_element_type=F32)
        ya = _causal_conv(ua, tail_ref[0, :, cols_a], buf_ref.at[c % 2, 0], cw_ref, cb_ref, cols_a)
        yg = _causal_conv(ug, tail_ref[0, :, cols_g], buf_ref.at[c % 2, 1], cw_ref, cb_ref, cols_g)
        tail_ref[0, :, cols_a] = ua[tm - SUBLANES:]
        tail_ref[0, :, cols_g] = ug[tm - SUBLANES:]
        act_ref[:, cols_a] = (_gelu_tanh(yg) * ya).astype(BF16)
    f = jnp.dot(act_ref[...], wdn_ref[...], preferred_element_type=F32)
    xo_ref[0] = _layer_norm(DEEPNORM_ALPHA * x1 + (1.0 + gf_ref[0]) * f, lng_ref[1:2], lnb_ref[1:2])


def _post(x, o, wo, gm, scf, shf, gf, lng, lnb, wup, cw, cb, wdn, prev8, *, tm, o_transposed):
    B, T, D = x.shape
    vec = pl.BlockSpec((1, 1, D), lambda b, t: (b, 0, 0))
    if o_transposed:
        o_spec = pl.BlockSpec((1, N_SLABS, LANES, tm), lambda b, t: (b, 0, 0, t))
    else:
        o_spec = pl.BlockSpec((1, N_SLABS, tm, LANES), lambda b, t: (b, 0, t, 0))
    return pl.pallas_call(
        functools.partial(_post_kernel, o_transposed=o_transposed),
        out_shape=[jax.ShapeDtypeStruct((B, T, D), F32),
                   jax.ShapeDtypeStruct((B, SUBLANES, 2 * D_FF), F32)],
        grid=(B, T // tm),
        in_specs=[pl.BlockSpec((1, tm, D), lambda b, t: (b, t, 0)),
                  o_spec,
                  _resident(wo.shape), vec, vec, vec, vec,
                  _resident(lng.shape), _resident(lnb.shape),
                  _resident(wup.shape), _resident(cw.shape), _resident(cb.shape), _resident(wdn.shape),
                  pl.BlockSpec((1, SUBLANES, 2 * D_FF), lambda b, t: (b, 0, 0))],
        out_specs=[pl.BlockSpec((1, tm, D), lambda b, t: (b, t, 0)),
                   pl.BlockSpec((1, SUBLANES, 2 * D_FF), lambda b, t: (b, 0, 0))],
        scratch_shapes=[pltpu.VMEM((tm, D_FF), BF16),
                        pltpu.VMEM((2, 2, FF_CHUNK // LANES, 2 * (SUBLANES + tm), LANES), F32)],
        compiler_params=_params(2),
        name="post",
    )(x, o, wo, gm, scf, shf, gf, lng, lnb, wup, cw, cb, wdn, prev8)


def _run_group(x, ada, ada_kv, caches, wts, biases, *, prompt):
    B, T, D = x.shape
    tm = min(T, 512)
    n_tiles = T // tm
    cache_a_k, cache_a_v, cache_b_k, cache_b_v, state_conv = caches
    new_ak, new_av, new_conv = [], [], []
    k_b = vt_b = b_k_rows = b_v_rows = None
    qa, ka = N_HEADS * HEAD_DIM, N_KV_A * HEAD_DIM
    nkv = 2 * N_HEADS * HEAD_DIM

    def slab_rows(cache):
        return jnp.swapaxes(cache.reshape(B, cache.shape[1], -1, LANES), 1, 2)

    def slab_cols(cache):
        return jnp.transpose(cache.reshape(B, cache.shape[1], -1, LANES), (0, 2, 3, 1))

    def attend(q, k, v, k_cache, v_cache, bias, sinks, *, sb, pb, group):
        if prompt:
            ratio = tm // pb
            prev = lambda t: jnp.maximum(t * ratio - 1, 0)
            return _attention(q, k, v, k, v, prev, bias, sinks,
                              tq=tm, sb=sb, group=group, first_tile_mask=True)
        pad = LANES - T
        q_t = jnp.pad(jnp.swapaxes(q, 2, 3), ((0, 0), (0, 0), (0, 0), (0, pad)))
        v_t = jnp.pad(jnp.swapaxes(v, 2, 3), ((0, 0), (0, 0), (0, 0), (0, pad)))
        k_p = jnp.pad(k, ((0, 0), (0, 0), (0, pad), (0, 0)))
        o_t = _attention(q_t, k_p, v_t, slab_rows(k_cache), slab_cols(v_cache), lambda t: 0, bias, sinks,
                         tq=LANES, sb=LANES, group=group, first_tile_mask=False)
        return jnp.swapaxes(o_t[..., :T], 2, 3)

    for l in range(DEPTH):
        vecs = [ada[l][:, i * D:(i + 1) * D].reshape(B, 1, D) for i in range(6)]
        sh_m, sc_m, g_m, sh_f, sc_f, g_f = vecs
        if l < N_A_LAYERS:
            pb = A_BACK * CHUNK
            groups = ((0, N_SLABS, prompt), (qa, KV_SLABS_A, False), (qa + ka, KV_SLABS_A, prompt))
            if prompt:
                q, k, v = _proj(x, sc_m, sh_m, wts["qkv"][l], tm=tm, n_tiles=n_tiles, groups=groups)
                rows = min(pb, T)
                kv_tail, = _proj(x, sc_m, sh_m, wts["qkv"][l][:, qa:], tm=rows, n_tiles=1,
                                 row_off=T // rows - 1, f32_cols=(0, 2 * ka))
                o = attend(q, k, v, None, None, biases["a"], wts["sinks"][l], sb=2 * CHUNK, pb=pb, group=4)
            else:
                q, k, v, kv_tail = _proj(x, sc_m, sh_m, wts["qkv"][l], tm=tm, n_tiles=n_tiles, groups=groups,
                                         f32_cols=(qa, qa + 2 * ka))
                o = attend(q, k, v, cache_a_k[l], cache_a_v[l], biases["a"], wts["sinks"][l],
                           sb=LANES, pb=cache_a_k.shape[2], group=4)
            new_ak.append(kv_tail[..., :ka].reshape(B, -1, N_KV_A, HEAD_DIM))
            new_av.append(kv_tail[..., ka:].reshape(B, -1, N_KV_A, HEAD_DIM))
            wo = wts["o_a"][l]
        else:
            j = l - N_A_LAYERS
            pb = B_BACK * CHUNK
            q, = _proj(x, sc_m, sh_m, wts["q_b"][j], tm=tm, n_tiles=n_tiles, groups=((0, N_SLABS, prompt),))
            if prompt:
                o = attend(q, k_b, vt_b, None, None, biases["b"][j], None, sb=4 * CHUNK, pb=pb, group=1)
            else:
                o = attend(q, k_b, vt_b, cache_b_k, cache_b_v, biases["b"][j], None,
                           sb=LANES, pb=cache_b_k.shape[1], group=1)
            wo = wts["o_b"][j]
        if prompt:
            prev8 = jnp.zeros((B, SUBLANES, 2 * D_FF), F32)
        else:
            prev8 = jnp.pad(state_conv[l], ((0, 0), (SUBLANES - (CONV_W - 1), 0), (0, 0)))
        x, tail8 = _post(x, o, wo, g_m, sc_f, sh_f, g_f, wts["ln_g"][l], wts["ln_b"][l],
                         wts["up"][l], wts["conv_w"][l], wts["conv_b"][l], wts["down"][l], prev8,
                         tm=tm, o_transposed=prompt)
        new_conv.append(tail8[:, SUBLANES - (CONV_W - 1):])
        if l == N_A_LAYERS - 1:
            sh_kv = ada_kv[:, :D].reshape(B, 1, D)
            sc_kv = ada_kv[:, D:].reshape(B, 1, D)
            groups = ((0, N_SLABS, False), (nkv // 2, N_SLABS, prompt))
            if prompt:
                k_b, vt_b = _proj(x, sc_kv, sh_kv, wts["kv_b"], tm=tm, n_tiles=n_tiles, groups=groups)
                rows = min(B_BACK * CHUNK, T)
                kv_rows, = _proj(x, sc_kv, sh_kv, wts["kv_b"], tm=rows, n_tiles=1,
                                 row_off=T // rows - 1, f32_cols=(0, nkv))
            else:
                k_b, vt_b, kv_rows = _proj(x, sc_kv, sh_kv, wts["kv_b"], tm=tm, n_tiles=n_tiles,
                                           groups=groups, f32_cols=(0, nkv))
            b_k_rows = kv_rows[..., :nkv // 2].reshape(B, -1, N_HEADS, HEAD_DIM)
            b_v_rows = kv_rows[..., nkv // 2:].reshape(B, -1, N_HEADS, HEAD_DIM)
    return x, jnp.stack(new_ak), jnp.stack(new_av), b_k_rows, b_v_rows, jnp.stack(new_conv)


def _group_biases(t5_table, relpos_b, *, prompt, T, rows_a, rows_b):
    def build(table_t, index_fn, n_back, sb, pb, heads_per_unit):
        if prompt:
            valid = _band_valid(np.arange(sb) + pb, np.arange(sb + pb), n_back)
        else:
            qpos = PAST_LEN + np.arange(sb)
            kpos = PAST_LEN - pb + np.arange(sb + pb)
            valid = _band_valid(qpos, kpos, n_back) & (kpos[None, :] >= 0)
            valid[T:, :] = True
            valid[:, pb + T:] = False
        neg_t = np.where(valid, 0.0, NEG_INF).astype(np.float32).T
        return _toeplitz_bias(table_t, index_fn, neg_t, pb, heads_per_unit)

    if prompt:
        a = build(t5_table, _t5_bucket, A_BACK, 2 * CHUNK, A_BACK * CHUNK, 4)
        b = [build(relpos_b[j].T, _relclip_index, B_BACK, 4 * CHUNK, B_BACK * CHUNK, 2)
             for j in range(relpos_b.shape[0])]
    else:
        a = build(t5_table, _t5_bucket, A_BACK, LANES, rows_a, 4)
        b = [build(relpos_b[j].T, _relclip_index, B_BACK, LANES, rows_b, 2)
             for j in range(relpos_b.shape[0])]
    return {"a": a, "b": b}


def kernel(x_prompt, x_sample, c_prompt, c_sample, cache_a_k, cache_a_v, cache_b_k, cache_b_v, state_conv, w_ada, b_ada, ln_g, ln_b, w_qkv_a, w_o_a, sinks_a, t5_table, w_ada_kv, b_ada_kv, w_kv_b, w_q_b, w_o_b, relpos_b, w_up, conv_w, conv_b, w_down):
    B = x_prompt.shape[0]
    qa = N_HEADS * HEAD_DIM
    qscale = jnp.concatenate([jnp.full((qa,), ATTN_SCALE, F32),
                              jnp.ones((w_qkv_a.shape[2] - qa,), F32)])
    wts = {
        "qkv": (w_qkv_a * qscale).astype(BF16),
        "o_a": w_o_a.astype(BF16),
        "sinks": sinks_a,
        "kv_b": w_kv_b.astype(BF16),
        "q_b": (w_q_b * ATTN_SCALE).astype(BF16),
        "o_b": w_o_b.astype(BF16),
        "up": w_up.astype(BF16),
        "conv_w": conv_w,
        "conv_b": conv_b.reshape(DEPTH, 1, 2 * D_FF),
        "down": w_down.astype(BF16),
        "ln_g": ln_g,
        "ln_b": ln_b,
    }
    c_all = jnp.concatenate([c_prompt, c_sample], axis=0)
    ada = _ada(c_all, w_ada, b_ada, 1536)
    ada_kv = _ada(c_all, w_ada_kv[None], b_ada_kv[None], 1024)[0]

    bias_p = _group_biases(t5_table, relpos_b, prompt=True, T=x_prompt.shape[1], rows_a=0, rows_b=0)
    bias_s = _group_biases(t5_table, relpos_b, prompt=False, T=x_sample.shape[1],
                           rows_a=cache_a_k.shape[2], rows_b=cache_b_k.shape[1])

    y_p, p_ak, p_av, p_bk, p_bv, p_conv = _run_group(
        x_prompt, ada[:, :B], ada_kv[:B], (None,) * 5, wts, bias_p, prompt=True)
    y_s, s_ak, s_av, s_bk, s_bv, s_conv = _run_group(
        x_sample, ada[:, B:], ada_kv[B:], (cache_a_k, cache_a_v, cache_b_k, cache_b_v, state_conv),
        wts, bias_s, prompt=False)
    return (y_p, y_s, p_ak, p_av, p_bk, p_bv, p_conv, s_ak, s_av, s_bk, s_bv, s_conv)
```

```python
import functools
import math

import numpy as np
import jax
import jax.numpy as jnp
from jax import lax
from jax.experimental import pallas as pl
from jax.experimental.pallas import tpu as pltpu

F32 = jnp.float32
BF16 = jnp.bfloat16

D_MODEL = 1024
DEPTH = 4
PAST_LEN = 2048
CHUNK = 64
HEAD_DIM = 64
N_A_LAYERS = DEPTH // 2
N_HEADS = 16
N_KV_A = 4
A_BACK = 2
B_BACK = 8
REL_CLIP = 128
N_BUCKETS = 32
T5_MAX_DIST = 128
D_FF = 2816
CONV_W = 3
LN_EPS = 1e-5
DEEPNORM_ALPHA = (2.0 * DEPTH) ** 0.25
ATTN_SCALE = HEAD_DIM ** -0.5
LOG2_E = math.log2(math.e)
NEG_INF = -1e30

LANES = 128
SUBLANES = 8
N_SLABS = N_HEADS * HEAD_DIM // LANES
KV_SLABS_A = N_KV_A * HEAD_DIM // LANES
FF_CHUNK = 256
VMEM_LIMIT = 56 * 1024 * 1024


def _params(n_axes):
    return pltpu.CompilerParams(dimension_semantics=("arbitrary",) * n_axes,
                                vmem_limit_bytes=VMEM_LIMIT)


def _resident(shape):
    nd = len(shape)
    return pl.BlockSpec(shape, lambda *_: (0,) * nd, pipeline_mode=pl.Buffered(1))


def _ada_kernel(c_ref, w_ref, b_ref, o_ref):
    sc = jax.nn.silu(c_ref[...])
    o_ref[0] = jnp.dot(sc.astype(BF16), w_ref[0].astype(BF16),
                       preferred_element_type=F32) + b_ref[0]


def _ada(c_all, w, b, nb):
    L, D, N = w.shape
    M = c_all.shape[0]
    return pl.pallas_call(
        _ada_kernel,
        out_shape=jax.ShapeDtypeStruct((L, M, N), F32),
        grid=(L, N // nb),
        in_specs=[pl.BlockSpec((M, D), lambda l, n: (0, 0)),
                  pl.BlockSpec((1, D, nb), lambda l, n: (l, 0, n)),
                  pl.BlockSpec((1, 1, nb), lambda l, n: (l, 0, n))],
        out_specs=pl.BlockSpec((1, M, nb), lambda l, n: (l, 0, n)),
        compiler_params=_params(2),
        name="ada",
    )(c_all, w, b.reshape(L, 1, N))


def _proj_kernel(x_ref, sc_ref, sh_ref, w_ref, *out_refs, groups, f32_cols):
    h = (x_ref[0] * (1.0 + sc_ref[0]) + sh_ref[0]).astype(BF16)
    y = jnp.dot(h, w_ref[...], preferred_element_type=F32)
    outs = list(out_refs)
    for col0, n_slabs, transposed in groups:
        o = outs.pop(0)
        for p in range(n_slabs):
            blk = y[:, col0 + p * LANES:col0 + (p + 1) * LANES]
            if transposed:
                o[0, 0, p] = blk.T.astype(BF16)
            else:
                o[0, p] = blk.astype(BF16)
    if f32_cols is not None:
        outs.pop(0)[0] = y[:, f32_cols[0]:f32_cols[1]]


def _proj(x, sc, sh, w, *, tm, n_tiles, row_off=0, groups=(), f32_cols=None):
    B, _, D = x.shape
    N = w.shape[1]
    rows = tm * n_tiles
    out_shape, out_specs = [], []
    for _, n_slabs, transposed in groups:
        if transposed:
            out_shape.append(jax.ShapeDtypeStruct((B, n_tiles, n_slabs, LANES, tm), BF16))
            out_specs.append(pl.BlockSpec((1, 1, n_slabs, LANES, tm), lambda b, t: (b, t, 0, 0, 0)))
        else:
            out_shape.append(jax.ShapeDtypeStruct((B, n_slabs, rows, LANES), BF16))
            out_specs.append(pl.BlockSpec((1, n_slabs, tm, LANES), lambda b, t: (b, 0, t, 0)))
    if f32_cols is not None:
        nc = f32_cols[1] - f32_cols[0]
        out_shape.append(jax.ShapeDtypeStruct((B, rows, nc), F32))
        out_specs.append(pl.BlockSpec((1, tm, nc), lambda b, t: (b, t, 0)))
    return pl.pallas_call(
        functools.partial(_proj_kernel, groups=tuple(groups), f32_cols=f32_cols),
        out_shape=out_shape,
        grid=(B, n_tiles),
        in_specs=[pl.BlockSpec((1, tm, D), lambda b, t: (b, t + row_off, 0)),
                  pl.BlockSpec((1, 1, D), lambda b, t: (b, 0, 0)),
                  pl.BlockSpec((1, 1, D), lambda b, t: (b, 0, 0)),
                  _resident((D, N))],
        out_specs=out_specs,
        compiler_params=_params(2),
        name="proj",
    )(x, sc, sh, w)


def _bias_kernel(tab_ref, idx_ref, neg_ref, o_ref, *, n_entries):
    h = pl.program_id(0)
    idx = idx_ref[...]
    n_keys, n_q = o_ref.shape[1], o_ref.shape[2]

    def pick(e, acc):
        return jnp.where(idx == e, tab_ref[e, h], acc)

    base = lax.fori_loop(0, n_entries, pick, jnp.zeros(idx.shape, F32))
    full = jnp.broadcast_to(base[0:1], (n_keys, idx.shape[1]))
    shifted = pltpu.roll(full, 0, 1, stride=1, stride_axis=0)
    o_ref[0] = shifted[:, :n_q] + neg_ref[...]


def _toeplitz_bias(table_t, idx_of_offset, neg_mask_t, back_rows, heads_per_unit):
    n_entries, H = table_t.shape
    KR, R = neg_mask_t.shape
    width = -(-(KR + R) // LANES) * LANES
    pos = np.arange(width)
    q_minus_key_row = np.where(pos < R, pos, pos - width)
    offs = -q_minus_key_row - back_rows
    idx_row = jnp.broadcast_to(idx_of_offset(jnp.asarray(offs, jnp.int32))[None, :],
                               (SUBLANES, width)).astype(jnp.int32)
    return pl.pallas_call(
        functools.partial(_bias_kernel, n_entries=n_entries),
        out_shape=jax.ShapeDtypeStruct((H // heads_per_unit, KR, heads_per_unit * R), F32),
        grid=(H,),
        in_specs=[pl.BlockSpec(memory_space=pltpu.SMEM),
                  pl.BlockSpec((SUBLANES, width), lambda h: (0, 0)),
                  pl.BlockSpec((KR, R), lambda h: (0, 0))],
        out_specs=pl.BlockSpec((1, KR, R), lambda h: (h // heads_per_unit, 0, h % heads_per_unit)),
        compiler_params=_params(1),
        name="bias",
    )(table_t, idx_row, jnp.asarray(neg_mask_t, F32))


def _t5_bucket(rel):
    nb = N_BUCKETS // 2
    max_exact = nb // 2
    n = jnp.abs(rel)
    large = max_exact + (jnp.log(jnp.maximum(n, 1).astype(F32) / max_exact)
                         / math.log(T5_MAX_DIST / max_exact) * (nb - max_exact)).astype(jnp.int32)
    large = jnp.minimum(large, nb - 1)
    return jnp.where(rel > 0, nb, 0) + jnp.where(n < max_exact, n, large)


def _relclip_index(rel):
    return jnp.clip(-rel, -REL_CLIP, REL_CLIP) + REL_CLIP


def _band_valid(qpos, kpos, n_back):
    qc = qpos // CHUNK
    kc = kpos // CHUNK
    return (kc[None, :] <= qc[:, None]) & (kc[None, :] >= qc[:, None] - n_back)


def _attn_kernel(*refs, sb, pb, group, first_tile_mask, has_sink):
    refs = list(refs)
    sink_ref = refs.pop(0) if has_sink else None
    q_ref, kp_ref, kc_ref, vp_ref, vc_ref, bias_ref, o_ref, s_ref, e_ref, m_ref, inv_ref = refs
    tq = q_ref.shape[4]
    kr = sb + pb
    n_sub = tq // sb
    pairs_per_unit = 2 if group == 4 else 1
    heads_per_unit = 2 * pairs_per_unit
    cols = heads_per_unit * sb
    n_units = N_SLABS // pairs_per_unit
    units_per_trip = 1 if n_sub % 2 == 0 else 2
    items_per_trip = units_per_trip * n_sub
    tile = pl.program_id(1)
    lane = lax.broadcasted_iota(jnp.int32, (1, LANES), 1)
    dim_row = lax.broadcasted_iota(jnp.int32, (LANES, 1), 0)
    low_dims = dim_row < HEAD_DIM
    col_head = lax.broadcasted_iota(jnp.int32, (1, cols), 1) // sb

    def kv_slab(u):
        return u // 2 if group == 4 else u

    def key_rows(u, i):
        r0 = i * sb
        parts = []
        if r0 < pb:
            parts.append(kp_ref[0, kv_slab(u), r0:pb, :].astype(BF16))
        parts.append(kc_ref[0, kv_slab(u), max(r0 - pb, 0):r0 + kr - pb, :].astype(BF16))
        x = parts[0] if len(parts) == 1 else jnp.concatenate(parts, 0)
        if group == 4:
            half = u % 2
            own = (lane >= HEAD_DIM * half) & (lane < HEAD_DIM * (half + 1))
            x = jnp.where(own, x, pltpu.roll(x, HEAD_DIM, 1))
        return x

    def value_cols(u, i):
        r0 = i * sb
        parts = []
        if r0 < pb:
            parts.append(vp_ref[0, 0, kv_slab(u), :, r0:pb].astype(BF16))
        parts.append(vc_ref[0, 0, kv_slab(u), :, max(r0 - pb, 0):r0 + kr - pb].astype(BF16))
        x = parts[0] if len(parts) == 1 else jnp.concatenate(parts, 1)
        if group == 4:
            half = u % 2
            own = (dim_row >= HEAD_DIM * half) & (dim_row < HEAD_DIM * (half + 1))
            x = jnp.where(own, x, jnp.concatenate([x[HEAD_DIM:], x[:HEAD_DIM]], 0))
        return x

    def sink_row(u):
        row = jnp.zeros((1, cols), F32)
        for hh in range(heads_per_unit):
            row = jnp.where(col_head == hh, sink_ref[u * heads_per_unit + hh], row)
        return row

    def band_rows(c0):
        q0 = c0 % sb
        return q0, q0 + LANES + pb

    def scores(u, i, slot, masked):
        q_cols = []
        for pi in range(pairs_per_unit):
            qp = q_ref[0, 0, u * pairs_per_unit + pi, :, i * sb:(i + 1) * sb]
            q_cols += [jnp.where(low_dims, qp, jnp.zeros_like(qp)),
                       jnp.where(low_dims, jnp.zeros_like(qp), qp)]
        qs = jnp.concatenate(q_cols, 1)
        s = jnp.dot(key_rows(u, i), qs, preferred_element_type=F32)
        n_before_start = pb - i * sb if masked else 0
        sinks = sink_row(u) if has_sink else None
        for c0 in range(0, cols, LANES):
            lo, hi = band_rows(c0)
            sc = s[lo:hi, c0:c0 + LANES] + bias_ref[u, lo:hi, c0:c0 + LANES]
            if n_before_start > lo:
                sc = jnp.concatenate([sc[:n_before_start - lo] + NEG_INF, sc[n_before_start - lo:]], 0)
            m = jnp.max(sc, axis=0, keepdims=True)
            if has_sink:
                m = jnp.maximum(m, sinks[:, c0:c0 + LANES])
            s_ref[slot, lo:hi, c0:c0 + LANES] = sc
            m_ref[slot, :, c0:c0 + LANES] = jnp.broadcast_to(m, (SUBLANES, LANES))

    def softmax(u, i, slot):
        sinks = sink_row(u) if has_sink else None
        for c0 in range(0, cols, LANES):
            lo, hi = band_rows(c0)
            m = m_ref[slot, 0:1, c0:c0 + LANES]
            e = jnp.exp2(s_ref[slot, lo:hi, c0:c0 + LANES] - m)
            denom = jnp.sum(e, axis=0, keepdims=True)
            if has_sink:
                denom = denom + jnp.exp2(sinks[:, c0:c0 + LANES] - m)
            e_ref[slot, lo:hi, c0:c0 + LANES] = e.astype(BF16)
            inv_ref[slot, :, c0:c0 + LANES] = jnp.broadcast_to(1.0 / denom, (SUBLANES, LANES))

    def values(u, i, slot):
        o = jnp.dot(value_cols(u, i), e_ref[slot], preferred_element_type=F32)
        o = o * inv_ref[slot, 0:1, :]
        for pi in range(pairs_per_unit):
            lo = o[:, (2 * pi) * sb:(2 * pi + 1) * sb]
            hi = o[:, (2 * pi + 1) * sb:(2 * pi + 2) * sb]
            o_ref[0, 0, u * pairs_per_unit + pi, :, i * sb:(i + 1) * sb] = jnp.where(low_dims, lo, hi).astype(BF16)

    def run(masked):
        e_ref[...] = jnp.zeros(e_ref.shape, BF16)
        inv_ref[1] = jnp.ones(inv_ref.shape[1:], F32)
        scores(0, 0, 0, masked)

        def trip(t, carry):
            for j in range(items_per_trip):
                u = t * units_per_trip + j // n_sub
                i = j % n_sub
                if j + 1 < items_per_trip:
                    nxt = (t * units_per_trip + (j + 1) // n_sub, (j + 1) % n_sub)
                else:
                    nxt = (jnp.minimum((t + 1) * units_per_trip, n_units - 1), 0)
                if j >= 1:
                    prv = (t * units_per_trip + (j - 1) // n_sub, (j - 1) % n_sub)
                else:
                    prv = (jnp.maximum(t * units_per_trip - 1, 0), n_sub - 1)
                scores(*nxt, (j + 1) % 2, masked)
                softmax(u, i, j % 2)
                values(*prv, (j + 1) % 2)
            return carry

        lax.fori_loop(0, n_units // units_per_trip, trip, 0)
        values(n_units - 1, n_sub - 1, (items_per_trip - 1) % 2)

    if first_tile_mask:
        pl.when(tile == 0)(lambda: run(True))
        pl.when(tile != 0)(lambda: run(False))
    else:
        run(False)


def _attention(q_t, k, v_t, k_prev, v_t_prev, prev_block, bias, sinks, *, tq, sb, group, first_tile_mask):
    B, n_tiles = q_t.shape[:2]
    kv_slabs = k.shape[1]
    pb = bias.shape[1] - sb
    cols = (4 if group == 4 else 2) * sb
    has_sink = sinks is not None
    k_prev_block, v_prev_block = prev_block
    in_specs = [pl.BlockSpec(memory_space=pltpu.SMEM)] if has_sink else []
    in_specs += [
        pl.BlockSpec((1, 1, N_SLABS, LANES, tq), lambda b, t: (b, t, 0, 0, 0)),
        pl.BlockSpec((1, kv_slabs, pb, LANES), lambda b, t: (b, 0, k_prev_block(t), 0)),
        pl.BlockSpec((1, kv_slabs, tq, LANES), lambda b, t: (b, 0, t, 0)),
        pl.BlockSpec((1, 1, kv_slabs, LANES, pb), lambda b, t: (b, v_prev_block(t)[0], 0, 0, v_prev_block(t)[1])),
        pl.BlockSpec((1, 1, kv_slabs, LANES, tq), lambda b, t: (b, t, 0, 0, 0)),
        _resident(bias.shape),
    ]
    args = ([sinks] if has_sink else []) + [q_t, k_prev, k, v_t_prev, v_t, bias]
    return pl.pallas_call(
        functools.partial(_attn_kernel, sb=sb, pb=pb, group=group,
                          first_tile_mask=first_tile_mask, has_sink=has_sink),
        out_shape=jax.ShapeDtypeStruct((B, n_tiles, N_SLABS, LANES, tq), BF16),
        grid=(B, n_tiles),
        in_specs=in_specs,
        out_specs=pl.BlockSpec((1, 1, N_SLABS, LANES, tq), lambda b, t: (b, t, 0, 0, 0)),
        scratch_shapes=[pltpu.VMEM((2, sb + pb, cols), F32),
                        pltpu.VMEM((2, sb + pb, cols), BF16),
                        pltpu.VMEM((2, SUBLANES, cols), F32),
                        pltpu.VMEM((2, SUBLANES, cols), F32)],
        compiler_params=_params(2),
        name="attn",
    )(*args)


def _layer_norm(y, g, b):
    mu = jnp.mean(y, axis=-1, keepdims=True)
    d = y - mu
    var = jnp.mean(d * d, axis=-1, keepdims=True)
    return d * lax.rsqrt(var + LN_EPS) * g + b


def _gelu_tanh(x):
    return 0.5 * x * (1.0 + jnp.tanh(math.sqrt(2.0 / math.pi) * (x + 0.044715 * (x * x * x))))


def _causal_conv(u, tail, buf_ref, cw_ref, cb_ref, cols):
    tm, n = u.shape
    parts = []
    for s in range(n // LANES):
        lanes = slice(s * LANES, (s + 1) * LANES)
        buf_ref[s, pl.ds(0, SUBLANES, stride=2), :] = tail[:, lanes]
        buf_ref[s, pl.ds(2 * SUBLANES, tm, stride=2), :] = u[:, lanes]
    for s in range(n // LANES):
        lanes = slice(cols.start + s * LANES, cols.start + (s + 1) * LANES)
        back1 = buf_ref[s, pl.ds(2 * SUBLANES - 2, tm, stride=2), :]
        back2 = buf_ref[s, pl.ds(2 * SUBLANES - 4, tm, stride=2), :]
        y = cb_ref[:, lanes] + cw_ref[0:1, lanes] * back2
        y = y + cw_ref[1:2, lanes] * back1
        parts.append(y + cw_ref[2:3, lanes] * u[:, s * LANES:(s + 1) * LANES])
    return jnp.concatenate(parts, axis=-1)


def _post_kernel(x_ref, o_ref, wo_ref, gm_ref, scf_ref, shf_ref, gf_ref, lng_ref, lnb_ref,
                 wup_ref, cw_ref, cb_ref, wdn_ref, prev_ref, xo_ref, tail_ref, act_ref, buf_ref,
                 *, o_transposed):
    tm = x_ref.shape[1]

    @pl.when(pl.program_id(1) == 0)
    def _():
        tail_ref[0] = prev_ref[0]

    if o_transposed:
        attn_t = jnp.concatenate([o_ref[0, 0, p] for p in range(N_SLABS)], axis=0)
        mix = lax.dot_general(attn_t, wo_ref[...], (((0,), (0,)), ((), ())), preferred_element_type=F32)
    else:
        attn = jnp.concatenate([o_ref[0, p] for p in range(N_SLABS)], axis=-1)
        mix = jnp.dot(attn, wo_ref[...], preferred_element_type=F32)
    x1 = _layer_norm(DEEPNORM_ALPHA * x_ref[0] + (1.0 + gm_ref[0]) * mix, lng_ref[0:1], lnb_ref[0:1])
    h = (x1 * (1.0 + scf_ref[0]) + shf_ref[0]).astype(BF16)
    for c in range(D_FF // FF_CHUNK):
        cols_a = slice(c * FF_CHUNK, (c + 1) * FF_CHUNK)
        cols_g = slice(D_FF + c * FF_CHUNK, D_FF + (c + 1) * FF_CHUNK)
        ua = jnp.dot(h, wup_ref[:, cols_a], preferred_element_type=F32)
        ug = jnp.dot(h, wup_ref[:, cols_g], preferred_element_type=F32)
        ya = _causal_conv(ua, tail_ref[0, :, cols_a], buf_ref.at[c % 2, 0], cw_ref, cb_ref, cols_a)
        yg = _causal_conv(ug, tail_ref[0, :, cols_g], buf_ref.at[c % 2, 1], cw_ref, cb_ref, cols_g)
        tail_ref[0, :, cols_a] = ua[tm - SUBLANES:]
        tail_ref[0, :, cols_g] = ug[tm - SUBLANES:]
        act_ref[:, cols_a] = (_gelu_tanh(yg) * ya).astype(BF16)
    f = jnp.dot(act_ref[...], wdn_ref[...], preferred_element_type=F32)
    xo_ref[0] = _layer_norm(DEEPNORM_ALPHA * x1 + (1.0 + gf_ref[0]) * f, lng_ref[1:2], lnb_ref[1:2])


def _post(x, o, wo, gm, scf, shf, gf, lng, lnb, wup, cw, cb, wdn, prev8, *, tm, o_transposed):
    B, T, D = x.shape
    vec = pl.BlockSpec((1, 1, D), lambda b, t: (b, 0, 0))
    if o_transposed:
        o_spec = pl.BlockSpec((1, 1, N_SLABS, LANES, tm), lambda b, t: (b, t, 0, 0, 0))
    else:
        o_spec = pl.BlockSpec((1, N_SLABS, tm, LANES), lambda b, t: (b, 0, t, 0))
    return pl.pallas_call(
        functools.partial(_post_kernel, o_transposed=o_transposed),
        out_shape=[jax.ShapeDtypeStruct((B, T, D), F32),
                   jax.ShapeDtypeStruct((B, SUBLANES, 2 * D_FF), F32)],
        grid=(B, T // tm),
        in_specs=[pl.BlockSpec((1, tm, D), lambda b, t: (b, t, 0)),
                  o_spec,
                  _resident(wo.shape), vec, vec, vec, vec,
                  _resident(lng.shape), _resident(lnb.shape),
                  _resident(wup.shape), _resident(cw.shape), _resident(cb.shape), _resident(wdn.shape),
                  pl.BlockSpec((1, SUBLANES, 2 * D_FF), lambda b, t: (b, 0, 0))],
        out_specs=[pl.BlockSpec((1, tm, D), lambda b, t: (b, t, 0)),
                   pl.BlockSpec((1, SUBLANES, 2 * D_FF), lambda b, t: (b, 0, 0))],
        scratch_shapes=[pltpu.VMEM((tm, D_FF), BF16),
                        pltpu.VMEM((2, 2, FF_CHUNK // LANES, 2 * (SUBLANES + tm), LANES), F32)],
        compiler_params=_params(2),
        name="post",
    )(x, o, wo, gm, scf, shf, gf, lng, lnb, wup, cw, cb, wdn, prev8)


def _run_group(x, ada, ada_kv, caches, wts, biases, *, prompt):
    B, T, D = x.shape
    tm = min(T, 512)
    n_tiles = T // tm
    cache_a_k, cache_a_v, cache_b_k, cache_b_v, state_conv = caches
    new_ak, new_av, new_conv = [], [], []
    k_b = vt_b = b_k_rows = b_v_rows = None
    qa, ka = N_HEADS * HEAD_DIM, N_KV_A * HEAD_DIM
    nkv = 2 * N_HEADS * HEAD_DIM

    def slab_rows(cache):
        return jnp.swapaxes(cache.reshape(B, cache.shape[1], -1, LANES), 1, 2)

    def slab_cols(cache):
        return jnp.transpose(cache.reshape(B, cache.shape[1], -1, LANES), (0, 2, 3, 1))

    def attend(q, k, v, k_cache, v_cache, bias, sinks, *, sb, pb, group):
        if prompt:
            ratio = tm // pb
            prev = (lambda t: jnp.maximum(t * ratio - 1, 0),
                    lambda t: (jnp.maximum(t - 1, 0), ratio - 1))
            return _attention(q, k, v, k, v, prev, bias, sinks,
                              tq=tm, sb=sb, group=group, first_tile_mask=True)
        pad = LANES - T
        q_t = jnp.pad(jnp.swapaxes(q, 2, 3), ((0, 0), (0, 0), (0, 0), (0, pad)))[:, None]
        v_t = jnp.pad(jnp.swapaxes(v, 2, 3), ((0, 0), (0, 0), (0, 0), (0, pad)))[:, None]
        k_p = jnp.pad(k, ((0, 0), (0, 0), (0, pad), (0, 0)))
        prev = (lambda t: 0, lambda t: (0, 0))
        o_t = _attention(q_t, k_p, v_t, slab_rows(k_cache), slab_cols(v_cache)[:, None], prev, bias, sinks,
                         tq=LANES, sb=LANES, group=group, first_tile_mask=False)
        return jnp.swapaxes(o_t[:, 0, :, :, :T], 2, 3)

    for l in range(DEPTH):
        vecs = [ada[l][:, i * D:(i + 1) * D].reshape(B, 1, D) for i in range(6)]
        sh_m, sc_m, g_m, sh_f, sc_f, g_f = vecs
        if l < N_A_LAYERS:
            pb = A_BACK * CHUNK
            groups = ((0, N_SLABS, prompt), (qa, KV_SLABS_A, False), (qa + ka, KV_SLABS_A, prompt))
            if prompt:
                q, k, v = _proj(x, sc_m, sh_m, wts["qkv"][l], tm=tm, n_tiles=n_tiles, groups=groups)
                rows = min(pb, T)
                kv_tail, = _proj(x, sc_m, sh_m, wts["qkv"][l][:, qa:], tm=rows, n_tiles=1,
                                 row_off=T // rows - 1, f32_cols=(0, 2 * ka))
                o = attend(q, k, v, None, None, biases["a"], wts["sinks"][l], sb=2 * CHUNK, pb=pb, group=4)
            else:
                q, k, v, kv_tail = _proj(x, sc_m, sh_m, wts["qkv"][l], tm=tm, n_tiles=n_tiles, groups=groups,
                                         f32_cols=(qa, qa + 2 * ka))
                o = attend(q, k, v, cache_a_k[l], cache_a_v[l], biases["a"], wts["sinks"][l],
                           sb=LANES, pb=cache_a_k.shape[2], group=4)
            new_ak.append(kv_tail[..., :ka].reshape(B, -1, N_KV_A, HEAD_DIM))
            new_av.append(kv_tail[..., ka:].reshape(B, -1, N_KV_A, HEAD_DIM))
            wo = wts["o_a"][l]
        else:
            j = l - N_A_LAYERS
            pb = B_BACK * CHUNK
            q, = _proj(x, sc_m, sh_m, wts["q_b"][j], tm=tm, n_tiles=n_tiles, groups=((0, N_SLABS, prompt),))
            if prompt:
                o = attend(q, k_b, vt_b, None, None, biases["b"][j], None, sb=4 * CHUNK, pb=pb, group=1)
            else:
                o = attend(q, k_b, vt_b, cache_b_k, cache_b_v, biases["b"][j], None,
                           sb=LANES, pb=cache_b_k.shape[1], group=1)
            wo = wts["o_b"][j]
        if prompt:
            prev8 = jnp.zeros((B, SUBLANES, 2 * D_FF), F32)
        else:
            prev8 = jnp.pad(state_conv[l], ((0, 0), (SUBLANES - (CONV_W - 1), 0), (0, 0)))
        x, tail8 = _post(x, o, wo, g_m, sc_f, sh_f, g_f, wts["ln_g"][l], wts["ln_b"][l],
                         wts["up"][l], wts["conv_w"][l], wts["conv_b"][l], wts["down"][l], prev8,
                         tm=tm, o_transposed=prompt)
        new_conv.append(tail8[:, SUBLANES - (CONV_W - 1):])
        if l == N_A_LAYERS - 1:
            sh_kv = ada_kv[:, :D].reshape(B, 1, D)
            sc_kv = ada_kv[:, D:].reshape(B, 1, D)
            groups = ((0, N_SLABS, False), (nkv // 2, N_SLABS, prompt))
            if prompt:
                k_b, vt_b = _proj(x, sc_kv, sh_kv, wts["kv_b"], tm=tm, n_tiles=n_tiles, groups=groups)
                rows = min(B_BACK * CHUNK, T)
                kv_rows, = _proj(x, sc_kv, sh_kv, wts["kv_b"], tm=rows, n_tiles=1,
                                 row_off=T // rows - 1, f32_cols=(0, nkv))
            else:
                k_b, vt_b, kv_rows = _proj(x, sc_kv, sh_kv, wts["kv_b"], tm=tm, n_tiles=n_tiles,
                                           groups=groups, f32_cols=(0, nkv))
            b_k_rows = kv_rows[..., :nkv // 2].reshape(B, -1, N_HEADS, HEAD_DIM)
            b_v_rows = kv_rows[..., nkv // 2:].reshape(B, -1, N_HEADS, HEAD_DIM)
    return x, jnp.stack(new_ak), jnp.stack(new_av), b_k_rows, b_v_rows, jnp.stack(new_conv)


def _group_biases(t5_table, relpos_b, *, prompt, T, rows_a, rows_b):
    def build(table_t, index_fn, n_back, sb, pb, heads_per_unit):
        if prompt:
            valid = _band_valid(np.arange(sb) + pb, np.arange(sb + pb), n_back)
        else:
            qpos = PAST_LEN + np.arange(sb)
            kpos = PAST_LEN - pb + np.arange(sb + pb)
            valid = _band_valid(qpos, kpos, n_back) & (kpos[None, :] >= 0)
            valid[T:, :] = True
            valid[:, pb + T:] = False
        neg_t = np.where(valid, 0.0, NEG_INF).astype(np.float32).T
        return _toeplitz_bias(table_t, index_fn, neg_t, pb, heads_per_unit)

    if prompt:
        a = build(t5_table, _t5_bucket, A_BACK, 2 * CHUNK, A_BACK * CHUNK, 4)
        b = [build(relpos_b[j].T, _relclip_index, B_BACK, 4 * CHUNK, B_BACK * CHUNK, 2)
             for j in range(relpos_b.shape[0])]
    else:
        a = build(t5_table, _t5_bucket, A_BACK, LANES, rows_a, 4)
        b = [build(relpos_b[j].T, _relclip_index, B_BACK, LANES, rows_b, 2)
             for j in range(relpos_b.shape[0])]
    return {"a": a, "b": b}


def kernel(x_prompt, x_sample, c_prompt, c_sample, cache_a_k, cache_a_v, cache_b_k, cache_b_v, state_conv, w_ada, b_ada, ln_g, ln_b, w_qkv_a, w_o_a, sinks_a, t5_table, w_ada_kv, b_ada_kv, w_kv_b, w_q_b, w_o_b, relpos_b, w_up, conv_w, conv_b, w_down):
    B = x_prompt.shape[0]
    qa = N_HEADS * HEAD_DIM
    qscale = jnp.concatenate([jnp.full((qa,), ATTN_SCALE * LOG2_E, F32),
                              jnp.ones((w_qkv_a.shape[2] - qa,), F32)])
    t5_table = t5_table * LOG2_E
    relpos_b = relpos_b * LOG2_E
    wts = {
        "qkv": (w_qkv_a * qscale).astype(BF16),
        "o_a": w_o_a.astype(BF16),
        "sinks": sinks_a * LOG2_E,
        "kv_b": w_kv_b.astype(BF16),
        "q_b": (w_q_b * (ATTN_SCALE * LOG2_E)).astype(BF16),
        "o_b": w_o_b.astype(BF16),
        "up": w_up.astype(BF16),
        "conv_w": conv_w,
        "conv_b": conv_b.reshape(DEPTH, 1, 2 * D_FF),
        "down": w_down.astype(BF16),
        "ln_g": ln_g,
        "ln_b": ln_b,
    }
    c_all = jnp.concatenate([c_prompt, c_sample], axis=0)
    ada = _ada(c_all, w_ada, b_ada, 1536)
    ada_kv = _ada(c_all, w_ada_kv[None], b_ada_kv[None], 1024)[0]

    bias_p = _group_biases(t5_table, relpos_b, prompt=True, T=x_prompt.shape[1], rows_a=0, rows_b=0)
    bias_s = _group_biases(t5_table, relpos_b, prompt=False, T=x_sample.shape[1],
                           rows_a=cache_a_k.shape[2], rows_b=cache_b_k.shape[1])

    y_p, p_ak, p_av, p_bk, p_bv, p_conv = _run_group(
        x_prompt, ada[:, :B], ada_kv[:B], (None,) * 5, wts, bias_p, prompt=True)
    y_s, s_ak, s_av, s_bk, s_bv, s_conv = _run_group(
        x_sample, ada[:, B:], ada_kv[B:], (cache_a_k, cache_a_v, cache_b_k, cache_b_v, state_conv),
        wts, bias_s, prompt=False)
    return (y_p, y_s, p_ak, p_av, p_bk, p_bv, p_conv, s_ak, s_av, s_bk, s_bv, s_conv)
```

```python
import functools
import math

import numpy as np
import jax
import jax.numpy as jnp
from jax import lax
from jax.experimental import pallas as pl
from jax.experimental.pallas import tpu as pltpu

F32 = jnp.float32
BF16 = jnp.bfloat16

D_MODEL = 1024
DEPTH = 4
PAST_LEN = 2048
CHUNK = 64
HEAD_DIM = 64
N_A_LAYERS = DEPTH // 2
N_HEADS = 16
N_KV_A = 4
A_BACK = 2
B_BACK = 8
REL_CLIP = 128
N_BUCKETS = 32
T5_MAX_DIST = 128
D_FF = 2816
CONV_W = 3
LN_EPS = 1e-5
DEEPNORM_ALPHA = (2.0 * DEPTH) ** 0.25
ATTN_SCALE = HEAD_DIM ** -0.5
LOG2_E = math.log2(math.e)
NEG_INF = -1e30

LANES = 128
SUBLANES = 8
N_SLABS = N_HEADS * HEAD_DIM // LANES
KV_SLABS_A = N_KV_A * HEAD_DIM // LANES
FF_CHUNK = 256
VMEM_LIMIT = 56 * 1024 * 1024


def _params(n_axes):
    return pltpu.CompilerParams(dimension_semantics=("arbitrary",) * n_axes,
                                vmem_limit_bytes=VMEM_LIMIT)


def _resident(shape):
    nd = len(shape)
    return pl.BlockSpec(shape, lambda *_: (0,) * nd, pipeline_mode=pl.Buffered(1))


def _ada_kernel(c_ref, w_ref, b_ref, o_ref):
    sc = jax.nn.silu(c_ref[...])
    o_ref[0] = jnp.dot(sc.astype(BF16), w_ref[0].astype(BF16),
                       preferred_element_type=F32) + b_ref[0]


def _ada(c_all, w, b, nb):
    L, D, N = w.shape
    M = c_all.shape[0]
    return pl.pallas_call(
        _ada_kernel,
        out_shape=jax.ShapeDtypeStruct((L, M, N), F32),
        grid=(L, N // nb),
        in_specs=[pl.BlockSpec((M, D), lambda l, n: (0, 0)),
                  pl.BlockSpec((1, D, nb), lambda l, n: (l, 0, n)),
                  pl.BlockSpec((1, 1, nb), lambda l, n: (l, 0, n))],
        out_specs=pl.BlockSpec((1, M, nb), lambda l, n: (l, 0, n)),
        compiler_params=_params(2),
        name="ada",
    )(c_all, w, b.reshape(L, 1, N))


def _proj_kernel(x_ref, sc_ref, sh_ref, w_ref, *out_refs, groups, f32_cols):
    h = (x_ref[0] * (1.0 + sc_ref[0]) + sh_ref[0]).astype(BF16)
    y = jnp.dot(h, w_ref[...], preferred_element_type=F32)
    outs = list(out_refs)
    for col0, n_slabs, transposed in groups:
        o = outs.pop(0)
        for p in range(n_slabs):
            blk = y[:, col0 + p * LANES:col0 + (p + 1) * LANES]
            if transposed:
                o[0, 0, p] = blk.T.astype(BF16)
            else:
                o[0, p] = blk.astype(BF16)
    if f32_cols is not None:
        outs.pop(0)[0] = y[:, f32_cols[0]:f32_cols[1]]


def _proj(x, sc, sh, w, *, tm, n_tiles, row_off=0, groups=(), f32_cols=None):
    B, _, D = x.shape
    N = w.shape[1]
    rows = tm * n_tiles
    out_shape, out_specs = [], []
    for _, n_slabs, transposed in groups:
        if transposed:
            out_shape.append(jax.ShapeDtypeStruct((B, n_tiles, n_slabs, LANES, tm), BF16))
            out_specs.append(pl.BlockSpec((1, 1, n_slabs, LANES, tm), lambda b, t: (b, t, 0, 0, 0)))
        else:
            out_shape.append(jax.ShapeDtypeStruct((B, n_slabs, rows, LANES), BF16))
            out_specs.append(pl.BlockSpec((1, n_slabs, tm, LANES), lambda b, t: (b, 0, t, 0)))
    if f32_cols is not None:
        nc = f32_cols[1] - f32_cols[0]
        out_shape.append(jax.ShapeDtypeStruct((B, rows, nc), F32))
        out_specs.append(pl.BlockSpec((1, tm, nc), lambda b, t: (b, t, 0)))
    return pl.pallas_call(
        functools.partial(_proj_kernel, groups=tuple(groups), f32_cols=f32_cols),
        out_shape=out_shape,
        grid=(B, n_tiles),
        in_specs=[pl.BlockSpec((1, tm, D), lambda b, t: (b, t + row_off, 0)),
                  pl.BlockSpec((1, 1, D), lambda b, t: (b, 0, 0)),
                  pl.BlockSpec((1, 1, D), lambda b, t: (b, 0, 0)),
                  _resident((D, N))],
        out_specs=out_specs,
        compiler_params=_params(2),
        name="proj",
    )(x, sc, sh, w)


def _bias_kernel(tab_ref, idx_ref, neg_ref, o_ref, *, n_entries):
    h = pl.program_id(0)
    idx = idx_ref[...]
    n_slabs, n_keys = o_ref.shape[1], o_ref.shape[2]

    def pick(e, acc):
        return jnp.where(idx == e, tab_ref[e, h], acc)

    base = lax.fori_loop(0, n_entries, pick, jnp.zeros(idx.shape, F32))
    full = jnp.broadcast_to(base[0:1], (n_keys, idx.shape[1]))
    shifted = pltpu.roll(full, 0, 1, stride=1, stride_axis=0)
    for j in range(n_slabs):
        o_ref[0, j] = shifted[:, j * LANES:(j + 1) * LANES] + neg_ref[:, j * LANES:(j + 1) * LANES]


def _toeplitz_bias(table_t, idx_of_offset, neg_mask_t, back_rows, heads_per_unit):
    n_entries, H = table_t.shape
    KR, R = neg_mask_t.shape
    width = -(-(KR + R) // LANES) * LANES
    pos = np.arange(width)
    q_minus_key_row = np.where(pos < R, pos, pos - width)
    offs = -q_minus_key_row - back_rows
    idx_row = jnp.broadcast_to(idx_of_offset(jnp.asarray(offs, jnp.int32))[None, :],
                               (SUBLANES, width)).astype(jnp.int32)
    return pl.pallas_call(
        functools.partial(_bias_kernel, n_entries=n_entries),
        out_shape=jax.ShapeDtypeStruct((H // heads_per_unit, heads_per_unit * R // LANES, KR, LANES), F32),
        grid=(H,),
        in_specs=[pl.BlockSpec(memory_space=pltpu.SMEM),
                  pl.BlockSpec((SUBLANES, width), lambda h: (0, 0)),
                  pl.BlockSpec((KR, R), lambda h: (0, 0))],
        out_specs=pl.BlockSpec((1, R // LANES, KR, LANES),
                               lambda h: (h // heads_per_unit, h % heads_per_unit, 0, 0)),
        compiler_params=_params(1),
        name="bias",
    )(table_t, idx_row, jnp.asarray(neg_mask_t, F32))


def _t5_bucket(rel):
    nb = N_BUCKETS // 2
    max_exact = nb // 2
    n = jnp.abs(rel)
    large = max_exact + (jnp.log(jnp.maximum(n, 1).astype(F32) / max_exact)
                         / math.log(T5_MAX_DIST / max_exact) * (nb - max_exact)).astype(jnp.int32)
    large = jnp.minimum(large, nb - 1)
    return jnp.where(rel > 0, nb, 0) + jnp.where(n < max_exact, n, large)


def _relclip_index(rel):
    return jnp.clip(-rel, -REL_CLIP, REL_CLIP) + REL_CLIP


def _band_valid(qpos, kpos, n_back):
    qc = qpos // CHUNK
    kc = kpos // CHUNK
    return (kc[None, :] <= qc[:, None]) & (kc[None, :] >= qc[:, None] - n_back)


def _attn_kernel(*refs, sb, pb, group, first_tile_mask, has_sink):
    refs = list(refs)
    sink_ref = refs.pop(0) if has_sink else None
    q_ref, kp_ref, kc_ref, vp_ref, vc_ref, bias_ref, o_ref, s_ref, e_ref, m_ref, inv_ref = refs
    tq = q_ref.shape[4]
    kr = sb + pb
    n_sub = tq // sb
    pairs_per_unit = 2 if group == 4 else 1
    heads_per_unit = 2 * pairs_per_unit
    cols = heads_per_unit * sb
    n_units = N_SLABS // pairs_per_unit
    units_per_trip = 1 if n_sub % 2 == 0 else 2
    items_per_trip = units_per_trip * n_sub
    tile = pl.program_id(1)
    lane = lax.broadcasted_iota(jnp.int32, (1, LANES), 1)
    dim_row = lax.broadcasted_iota(jnp.int32, (LANES, 1), 0)
    low_dims = dim_row < HEAD_DIM
    col_head = lax.broadcasted_iota(jnp.int32, (1, cols), 1) // sb

    def kv_slab(u):
        return u // 2 if group == 4 else u

    def key_rows(u, i):
        r0 = i * sb
        parts = []
        if r0 < pb:
            parts.append(kp_ref[0, kv_slab(u), r0:pb, :].astype(BF16))
        parts.append(kc_ref[0, kv_slab(u), max(r0 - pb, 0):r0 + kr - pb, :].astype(BF16))
        x = parts[0] if len(parts) == 1 else jnp.concatenate(parts, 0)
        if group == 4:
            half = u % 2
            own = (lane >= HEAD_DIM * half) & (lane < HEAD_DIM * (half + 1))
            x = jnp.where(own, x, pltpu.roll(x, HEAD_DIM, 1))
        return x

    def value_cols(u, i):
        r0 = i * sb
        parts = []
        if r0 < pb:
            parts.append(vp_ref[0, 0, kv_slab(u), :, r0:pb].astype(BF16))
        parts.append(vc_ref[0, 0, kv_slab(u), :, max(r0 - pb, 0):r0 + kr - pb].astype(BF16))
        x = parts[0] if len(parts) == 1 else jnp.concatenate(parts, 1)
        if group == 4:
            half = u % 2
            own = (dim_row >= HEAD_DIM * half) & (dim_row < HEAD_DIM * (half + 1))
            x = jnp.where(own, x, jnp.concatenate([x[HEAD_DIM:], x[:HEAD_DIM]], 0))
        return x

    def sink_row(u):
        row = jnp.zeros((1, cols), F32)
        for hh in range(heads_per_unit):
            row = jnp.where(col_head == hh, sink_ref[u * heads_per_unit + hh], row)
        return row

    def band_rows(c0):
        q0 = c0 % sb
        return q0, q0 + LANES + pb

    def scores(u, i, slot, masked):
        q_cols = []
        for pi in range(pairs_per_unit):
            qp = q_ref[0, 0, u * pairs_per_unit + pi, :, i * sb:(i + 1) * sb]
            q_cols += [jnp.where(low_dims, qp, jnp.zeros_like(qp)),
                       jnp.where(low_dims, jnp.zeros_like(qp), qp)]
        qs = jnp.concatenate(q_cols, 1)
        s = jnp.dot(key_rows(u, i), qs, preferred_element_type=F32)
        n_before_start = pb - i * sb if masked else 0
        sinks = sink_row(u) if has_sink else None
        for c0 in range(0, cols, LANES):
            lo, hi = band_rows(c0)
            sc = s[lo:hi, c0:c0 + LANES] + bias_ref[u, c0 // LANES, lo:hi, :]
            if n_before_start > lo:
                sc = jnp.concatenate([sc[:n_before_start - lo] + NEG_INF, sc[n_before_start - lo:]], 0)
            m = jnp.max(sc, axis=0, keepdims=True)
            if has_sink:
                m = jnp.maximum(m, sinks[:, c0:c0 + LANES])
            s_ref[slot, c0 // LANES, lo:hi, :] = sc
            m_ref[slot, :, c0:c0 + LANES] = jnp.broadcast_to(m, (SUBLANES, LANES))

    def softmax(u, i, slot):
        sinks = sink_row(u) if has_sink else None
        for c0 in range(0, cols, LANES):
            lo, hi = band_rows(c0)
            m = m_ref[slot, 0:1, c0:c0 + LANES]
            e = jnp.exp2(s_ref[slot, c0 // LANES, lo:hi, :] - m)
            denom = jnp.sum(e, axis=0, keepdims=True)
            if has_sink:
                denom = denom + jnp.exp2(sinks[:, c0:c0 + LANES] - m)
            e_ref[slot, c0 // LANES, lo:hi, :] = e.astype(BF16)
            inv_ref[slot, :, c0:c0 + LANES] = jnp.broadcast_to(1.0 / denom, (SUBLANES, LANES))

    def values(u, i, slot):
        e = jnp.concatenate([e_ref[slot, c] for c in range(cols // LANES)], axis=1)
        o = jnp.dot(value_cols(u, i), e, preferred_element_type=F32)
        o = o * inv_ref[slot, 0:1, :]
        for pi in range(pairs_per_unit):
            lo = o[:, (2 * pi) * sb:(2 * pi + 1) * sb]
            hi = o[:, (2 * pi + 1) * sb:(2 * pi + 2) * sb]
            o_ref[0, 0, u * pairs_per_unit + pi, :, i * sb:(i + 1) * sb] = jnp.where(low_dims, lo, hi).astype(BF16)

    def run(masked):
        e_ref[...] = jnp.zeros(e_ref.shape, BF16)
        inv_ref[1] = jnp.ones(inv_ref.shape[1:], F32)
        scores(0, 0, 0, masked)

        def trip(t, carry):
            for j in range(items_per_trip):
                u = t * units_per_trip + j // n_sub
                i = j % n_sub
                if j + 1 < items_per_trip:
                    nxt = (t * units_per_trip + (j + 1) // n_sub, (j + 1) % n_sub)
                else:
                    nxt = (jnp.minimum((t + 1) * units_per_trip, n_units - 1), 0)
                if j >= 1:
                    prv = (t * units_per_trip + (j - 1) // n_sub, (j - 1) % n_sub)
                else:
                    prv = (jnp.maximum(t * units_per_trip - 1, 0), n_sub - 1)
                scores(*nxt, (j + 1) % 2, masked)
                softmax(u, i, j % 2)
                values(*prv, (j + 1) % 2)
            return carry

        lax.fori_loop(0, n_units // units_per_trip, trip, 0)
        values(n_units - 1, n_sub - 1, (items_per_trip - 1) % 2)

    if first_tile_mask:
        pl.when(tile == 0)(lambda: run(True))
        pl.when(tile != 0)(lambda: run(False))
    else:
        run(False)


def _attention(q_t, k, v_t, k_prev, v_t_prev, prev_block, bias, sinks, *, tq, sb, group, first_tile_mask):
    B, n_tiles = q_t.shape[:2]
    kv_slabs = k.shape[1]
    pb = bias.shape[2] - sb
    cols = (4 if group == 4 else 2) * sb
    col_slabs = cols // LANES
    has_sink = sinks is not None
    k_prev_block, v_prev_block = prev_block
    in_specs = [pl.BlockSpec(memory_space=pltpu.SMEM)] if has_sink else []
    in_specs += [
        pl.BlockSpec((1, 1, N_SLABS, LANES, tq), lambda b, t: (b, t, 0, 0, 0)),
        pl.BlockSpec((1, kv_slabs, pb, LANES), lambda b, t: (b, 0, k_prev_block(t), 0)),
        pl.BlockSpec((1, kv_slabs, tq, LANES), lambda b, t: (b, 0, t, 0)),
        pl.BlockSpec((1, 1, kv_slabs, LANES, pb), lambda b, t: (b, v_prev_block(t)[0], 0, 0, v_prev_block(t)[1])),
        pl.BlockSpec((1, 1, kv_slabs, LANES, tq), lambda b, t: (b, t, 0, 0, 0)),
        _resident(bias.shape),
    ]
    args = ([sinks] if has_sink else []) + [q_t, k_prev, k, v_t_prev, v_t, bias]
    return pl.pallas_call(
        functools.partial(_attn_kernel, sb=sb, pb=pb, group=group,
                          first_tile_mask=first_tile_mask, has_sink=has_sink),
        out_shape=jax.ShapeDtypeStruct((B, n_tiles, N_SLABS, LANES, tq), BF16),
        grid=(B, n_tiles),
        in_specs=in_specs,
        out_specs=pl.BlockSpec((1, 1, N_SLABS, LANES, tq), lambda b, t: (b, t, 0, 0, 0)),
        scratch_shapes=[pltpu.VMEM((2, col_slabs, sb + pb, LANES), F32),
                        pltpu.VMEM((2, col_slabs, sb + pb, LANES), BF16),
                        pltpu.VMEM((2, SUBLANES, cols), F32),
                        pltpu.VMEM((2, SUBLANES, cols), F32)],
        compiler_params=_params(2),
        name="attn",
    )(*args)


def _layer_norm(y, g, b):
    mu = jnp.mean(y, axis=-1, keepdims=True)
    d = y - mu
    var = jnp.mean(d * d, axis=-1, keepdims=True)
    return d * lax.rsqrt(var + LN_EPS) * g + b


def _gelu_tanh(x):
    return 0.5 * x * (1.0 + jnp.tanh(math.sqrt(2.0 / math.pi) * (x + 0.044715 * (x * x * x))))


def _causal_conv(u, tail, buf_ref, cw_ref, cb_ref, cols):
    tm, n = u.shape
    parts = []
    for s in range(n // LANES):
        lanes = slice(s * LANES, (s + 1) * LANES)
        buf_ref[s, pl.ds(0, SUBLANES, stride=2), :] = tail[:, lanes]
        buf_ref[s, pl.ds(2 * SUBLANES, tm, stride=2), :] = u[:, lanes]
    for s in range(n // LANES):
        lanes = slice(cols.start + s * LANES, cols.start + (s + 1) * LANES)
        back1 = buf_ref[s, pl.ds(2 * SUBLANES - 2, tm, stride=2), :]
        back2 = buf_ref[s, pl.ds(2 * SUBLANES - 4, tm, stride=2), :]
        y = cb_ref[:, lanes] + cw_ref[0:1, lanes] * back2
        y = y + cw_ref[1:2, lanes] * back1
        parts.append(y + cw_ref[2:3, lanes] * u[:, s * LANES:(s + 1) * LANES])
    return jnp.concatenate(parts, axis=-1)


def _post_kernel(x_ref, o_ref, wo_ref, gm_ref, scf_ref, shf_ref, gf_ref, lng_ref, lnb_ref,
                 wup_ref, cw_ref, cb_ref, wdn_ref, prev_ref, xo_ref, tail_ref, act_ref, buf_ref,
                 *, o_transposed):
    tm = x_ref.shape[1]

    @pl.when(pl.program_id(1) == 0)
    def _():
        tail_ref[0] = prev_ref[0]

    n_half = buf_ref.shape[2]
    hr = tm // n_half
    halves = [slice(a * hr, (a + 1) * hr) for a in range(n_half)]
    if o_transposed:
        attn_t = jnp.concatenate([o_ref[0, 0, p] for p in range(N_SLABS)], axis=0)
        mixes = [lax.dot_general(attn_t[:, rows], wo_ref[...], (((0,), (0,)), ((), ())),
                                 preferred_element_type=F32) for rows in halves]
    else:
        attn = jnp.concatenate([o_ref[0, p] for p in range(N_SLABS)], axis=-1)
        mixes = [jnp.dot(attn[rows], wo_ref[...], preferred_element_type=F32) for rows in halves]
    x1s = [_layer_norm(DEEPNORM_ALPHA * x_ref[0, rows] + (1.0 + gm_ref[0]) * mix, lng_ref[0:1], lnb_ref[0:1])
           for rows, mix in zip(halves, mixes)]
    hs = [(x1 * (1.0 + scf_ref[0]) + shf_ref[0]).astype(BF16) for x1 in x1s]
    for c in range(D_FF // FF_CHUNK):
        cols_a = slice(c * FF_CHUNK, (c + 1) * FF_CHUNK)
        cols_g = slice(D_FF + c * FF_CHUNK, D_FF + (c + 1) * FF_CHUNK)
        for a, rows in enumerate(halves):
            ua = jnp.dot(hs[a], wup_ref[:, cols_a], preferred_element_type=F32)
            ug = jnp.dot(hs[a], wup_ref[:, cols_g], preferred_element_type=F32)
            ya = _causal_conv(ua, tail_ref[0, :, cols_a], buf_ref.at[c % 2, 0, a], cw_ref, cb_ref, cols_a)
            yg = _causal_conv(ug, tail_ref[0, :, cols_g], buf_ref.at[c % 2, 1, a], cw_ref, cb_ref, cols_g)
            tail_ref[0, :, cols_a] = ua[hr - SUBLANES:]
            tail_ref[0, :, cols_g] = ug[hr - SUBLANES:]
            act_ref[rows, cols_a] = (_gelu_tanh(yg) * ya).astype(BF16)
    for rows, x1 in zip(halves, x1s):
        f = jnp.dot(act_ref[rows, :], wdn_ref[...], preferred_element_type=F32)
        xo_ref[0, rows] = _layer_norm(DEEPNORM_ALPHA * x1 + (1.0 + gf_ref[0]) * f, lng_ref[1:2], lnb_ref[1:2])


def _post(x, o, wo, gm, scf, shf, gf, lng, lnb, wup, cw, cb, wdn, prev8, *, tm, o_transposed):
    B, T, D = x.shape
    n_half = 2 if tm % (2 * FF_CHUNK) == 0 else 1
    vec = pl.BlockSpec((1, 1, D), lambda b, t: (b, 0, 0))
    if o_transposed:
        o_spec = pl.BlockSpec((1, 1, N_SLABS, LANES, tm), lambda b, t: (b, t, 0, 0, 0))
    else:
        o_spec = pl.BlockSpec((1, N_SLABS, tm, LANES), lambda b, t: (b, 0, t, 0))
    return pl.pallas_call(
        functools.partial(_post_kernel, o_transposed=o_transposed),
        out_shape=[jax.ShapeDtypeStruct((B, T, D), F32),
                   jax.ShapeDtypeStruct((B, SUBLANES, 2 * D_FF), F32)],
        grid=(B, T // tm),
        in_specs=[pl.BlockSpec((1, tm, D), lambda b, t: (b, t, 0)),
                  o_spec,
                  _resident(wo.shape), vec, vec, vec, vec,
                  _resident(lng.shape), _resident(lnb.shape),
                  _resident(wup.shape), _resident(cw.shape), _resident(cb.shape), _resident(wdn.shape),
                  pl.BlockSpec((1, SUBLANES, 2 * D_FF), lambda b, t: (b, 0, 0))],
        out_specs=[pl.BlockSpec((1, tm, D), lambda b, t: (b, t, 0)),
                   pl.BlockSpec((1, SUBLANES, 2 * D_FF), lambda b, t: (b, 0, 0))],
        scratch_shapes=[pltpu.VMEM((tm, D_FF), BF16),
                        pltpu.VMEM((2, 2, n_half, FF_CHUNK // LANES, 2 * (SUBLANES + tm // n_half), LANES), F32)],
        compiler_params=_params(2),
        name="post",
    )(x, o, wo, gm, scf, shf, gf, lng, lnb, wup, cw, cb, wdn, prev8)


def _run_group(x, ada, ada_kv, caches, wts, biases, *, prompt):
    B, T, D = x.shape
    tm = min(T, 512)
    n_tiles = T // tm
    cache_a_k, cache_a_v, cache_b_k, cache_b_v, state_conv = caches
    new_ak, new_av, new_conv = [], [], []
    k_b = vt_b = b_k_rows = b_v_rows = None
    qa, ka = N_HEADS * HEAD_DIM, N_KV_A * HEAD_DIM
    nkv = 2 * N_HEADS * HEAD_DIM

    def slab_rows(cache):
        return jnp.swapaxes(cache.reshape(B, cache.shape[1], -1, LANES), 1, 2)

    def slab_cols(cache):
        return jnp.transpose(cache.reshape(B, cache.shape[1], -1, LANES), (0, 2, 3, 1))

    def attend(q, k, v, k_cache, v_cache, bias, sinks, *, sb, pb, group):
        if prompt:
            ratio = tm // pb
            prev = (lambda t: jnp.maximum(t * ratio - 1, 0),
                    lambda t: (jnp.maximum(t - 1, 0), ratio - 1))
            return _attention(q, k, v, k, v, prev, bias, sinks,
                              tq=tm, sb=sb, group=group, first_tile_mask=True)
        pad = LANES - T
        q_t = jnp.pad(jnp.swapaxes(q, 2, 3), ((0, 0), (0, 0), (0, 0), (0, pad)))[:, None]
        v_t = jnp.pad(jnp.swapaxes(v, 2, 3), ((0, 0), (0, 0), (0, 0), (0, pad)))[:, None]
        k_p = jnp.pad(k, ((0, 0), (0, 0), (0, pad), (0, 0)))
        prev = (lambda t: 0, lambda t: (0, 0))
        o_t = _attention(q_t, k_p, v_t, slab_rows(k_cache), slab_cols(v_cache)[:, None], prev, bias, sinks,
                         tq=LANES, sb=LANES, group=group, first_tile_mask=False)
        return jnp.swapaxes(o_t[:, 0, :, :, :T], 2, 3)

    for l in range(DEPTH):
        vecs = [ada[l][:, i * D:(i + 1) * D].reshape(B, 1, D) for i in range(6)]
        sh_m, sc_m, g_m, sh_f, sc_f, g_f = vecs
        if l < N_A_LAYERS:
            pb = A_BACK * CHUNK
            groups = ((0, N_SLABS, prompt), (qa, KV_SLABS_A, False), (qa + ka, KV_SLABS_A, prompt))
            if prompt:
                q, k, v = _proj(x, sc_m, sh_m, wts["qkv"][l], tm=tm, n_tiles=n_tiles, groups=groups)
                rows = min(pb, T)
                kv_tail, = _proj(x, sc_m, sh_m, wts["qkv"][l][:, qa:], tm=rows, n_tiles=1,
                                 row_off=T // rows - 1, f32_cols=(0, 2 * ka))
                o = attend(q, k, v, None, None, biases["a"], wts["sinks"][l], sb=2 * CHUNK, pb=pb, group=4)
            else:
                q, k, v, kv_tail = _proj(x, sc_m, sh_m, wts["qkv"][l], tm=tm, n_tiles=n_tiles, groups=groups,
                                         f32_cols=(qa, qa + 2 * ka))
                o = attend(q, k, v, cache_a_k[l], cache_a_v[l], biases["a"], wts["sinks"][l],
                           sb=LANES, pb=cache_a_k.shape[2], group=4)
            new_ak.append(kv_tail[..., :ka].reshape(B, -1, N_KV_A, HEAD_DIM))
            new_av.append(kv_tail[..., ka:].reshape(B, -1, N_KV_A, HEAD_DIM))
            wo = wts["o_a"][l]
        else:
            j = l - N_A_LAYERS
            pb = B_BACK * CHUNK
            q, = _proj(x, sc_m, sh_m, wts["q_b"][j], tm=tm, n_tiles=n_tiles, groups=((0, N_SLABS, prompt),))
            if prompt:
                o = attend(q, k_b, vt_b, None, None, biases["b"][j], None, sb=4 * CHUNK, pb=pb, group=1)
            else:
                o = attend(q, k_b, vt_b, cache_b_k, cache_b_v, biases["b"][j], None,
                           sb=LANES, pb=cache_b_k.shape[1], group=1)
            wo = wts["o_b"][j]
        if prompt:
            prev8 = jnp.zeros((B, SUBLANES, 2 * D_FF), F32)
        else:
            prev8 = jnp.pad(state_conv[l], ((0, 0), (SUBLANES - (CONV_W - 1), 0), (0, 0)))
        x, tail8 = _post(x, o, wo, g_m, sc_f, sh_f, g_f, wts["ln_g"][l], wts["ln_b"][l],
                         wts["up"][l], wts["conv_w"][l], wts["conv_b"][l], wts["down"][l], prev8,
                         tm=tm, o_transposed=prompt)
        new_conv.append(tail8[:, SUBLANES - (CONV_W - 1):])
        if l == N_A_LAYERS - 1:
            sh_kv = ada_kv[:, :D].reshape(B, 1, D)
            sc_kv = ada_kv[:, D:].reshape(B, 1, D)
            groups = ((0, N_SLABS, False), (nkv // 2, N_SLABS, prompt))
            if prompt:
                k_b, vt_b = _proj(x, sc_kv, sh_kv, wts["kv_b"], tm=tm, n_tiles=n_tiles, groups=groups)
                rows = min(B_BACK * CHUNK, T)
                kv_rows, = _proj(x, sc_kv, sh_kv, wts["kv_b"], tm=rows, n_tiles=1,
                                 row_off=T // rows - 1, f32_cols=(0, nkv))
            else:
                k_b, vt_b, kv_rows = _proj(x, sc_kv, sh_kv, wts["kv_b"], tm=tm, n_tiles=n_tiles,
                                           groups=groups, f32_cols=(0, nkv))
            b_k_rows = kv_rows[..., :nkv // 2].reshape(B, -1, N_HEADS, HEAD_DIM)
            b_v_rows = kv_rows[..., nkv // 2:].reshape(B, -1, N_HEADS, HEAD_DIM)
    return x, jnp.stack(new_ak), jnp.stack(new_av), b_k_rows, b_v_rows, jnp.stack(new_conv)


def _group_biases(t5_table, relpos_b, *, prompt, T, rows_a, rows_b):
    def build(table_t, index_fn, n_back, sb, pb, heads_per_unit):
        if prompt:
            valid = _band_valid(np.arange(sb) + pb, np.arange(sb + pb), n_back)
        else:
            qpos = PAST_LEN + np.arange(sb)
            kpos = PAST_LEN - pb + np.arange(sb + pb)
            valid = _band_valid(qpos, kpos, n_back) & (kpos[None, :] >= 0)
            valid[T:, :] = True
            valid[:, pb + T:] = False
        neg_t = np.where(valid, 0.0, NEG_INF).astype(np.float32).T
        return _toeplitz_bias(table_t, index_fn, neg_t, pb, heads_per_unit)

    if prompt:
        a = build(t5_table, _t5_bucket, A_BACK, 2 * CHUNK, A_BACK * CHUNK, 4)
        b = [build(relpos_b[j].T, _relclip_index, B_BACK, 4 * CHUNK, B_BACK * CHUNK, 2)
             for j in range(relpos_b.shape[0])]
    else:
        a = build(t5_table, _t5_bucket, A_BACK, LANES, rows_a, 4)
        b = [build(relpos_b[j].T, _relclip_index, B_BACK, LANES, rows_b, 2)
             for j in range(relpos_b.shape[0])]
    return {"a": a, "b": b}


def kernel(x_prompt, x_sample, c_prompt, c_sample, cache_a_k, cache_a_v, cache_b_k, cache_b_v, state_conv, w_ada, b_ada, ln_g, ln_b, w_qkv_a, w_o_a, sinks_a, t5_table, w_ada_kv, b_ada_kv, w_kv_b, w_q_b, w_o_b, relpos_b, w_up, conv_w, conv_b, w_down):
    B = x_prompt.shape[0]
    qa = N_HEADS * HEAD_DIM
    qscale = jnp.concatenate([jnp.full((qa,), ATTN_SCALE * LOG2_E, F32),
                              jnp.ones((w_qkv_a.shape[2] - qa,), F32)])
    t5_table = t5_table * LOG2_E
    relpos_b = relpos_b * LOG2_E
    wts = {
        "qkv": (w_qkv_a * qscale).astype(BF16),
        "o_a": w_o_a.astype(BF16),
        "sinks": sinks_a * LOG2_E,
        "kv_b": w_kv_b.astype(BF16),
        "q_b": (w_q_b * (ATTN_SCALE * LOG2_E)).astype(BF16),
        "o_b": w_o_b.astype(BF16),
        "up": w_up.astype(BF16),
        "conv_w": conv_w,
        "conv_b": conv_b.reshape(DEPTH, 1, 2 * D_FF),
        "down": w_down.astype(BF16),
        "ln_g": ln_g,
        "ln_b": ln_b,
    }
    c_all = jnp.concatenate([c_prompt, c_sample], axis=0)
    ada = _ada(c_all, w_ada, b_ada, 1536)
    ada_kv = _ada(c_all, w_ada_kv[None], b_ada_kv[None], 1024)[0]

    bias_p = _group_biases(t5_table, relpos_b, prompt=True, T=x_prompt.shape[1], rows_a=0, rows_b=0)
    bias_s = _group_biases(t5_table, relpos_b, prompt=False, T=x_sample.shape[1],
                           rows_a=cache_a_k.shape[2], rows_b=cache_b_k.shape[1])

    y_p, p_ak, p_av, p_bk, p_bv, p_conv = _run_group(
        x_prompt, ada[:, :B], ada_kv[:B], (None,) * 5, wts, bias_p, prompt=True)
    y_s, s_ak, s_av, s_bk, s_bv, s_conv = _run_group(
        x_sample, ada[:, B:], ada_kv[B:], (cache_a_k, cache_a_v, cache_b_k, cache_b_v, state_conv),
        wts, bias_s, prompt=False)
    return (y_p, y_s, p_ak, p_av, p_bk, p_bv, p_conv, s_ak, s_av, s_bk, s_bv, s_conv)
```
